```python
import math
import functools
import jax
import jax.numpy as jnp
from jax import lax
import numpy as np

D_MODEL = 1024
BATCH = 2
SEQ = 16384
DEPTH = 4

N_MIXERS = 4
CHUNK = 64
CONV_K = 5
PLE_DIM = 256

GDN_QK_HEADS = 8
GDN_V_HEADS = 16
GDN_HEAD_DIM = 128
GDN_KEY_WIDTH = GDN_QK_HEADS * GDN_HEAD_DIM
GDN_VAL_WIDTH = GDN_V_HEADS * GDN_HEAD_DIM
GDN_CONV_CH = 2 * GDN_KEY_WIDTH + GDN_VAL_WIDTH
GDN_IN = GDN_CONV_CH + GDN_VAL_WIDTH + 4 * GDN_V_HEADS

M2_D_INNER = 2 * D_MODEL
M2_HEAD_DIM = 64
M2_HEADS = M2_D_INNER // M2_HEAD_DIM
M2_GROUPS = 4
M2_HEADS_PER_GROUP = M2_HEADS // M2_GROUPS
M2_STATE = 128
M2_CONV_CH = M2_D_INNER + 2 * M2_GROUPS * M2_STATE
M2_IN = M2_D_INNER + M2_CONV_CH + 2 * M2_HEADS

HG_EXPAND = 128
HG_HEADS = D_MODEL // HG_EXPAND
HG_WIDTH = HG_HEADS * HG_EXPAND
HG_IN = 5 * HG_WIDTH

GLA_HEADS = 4
GLA_KEY_WIDTH = D_MODEL // 2
GLA_VAL_WIDTH = D_MODEL
GLA_K = GLA_KEY_WIDTH // GLA_HEADS
GLA_V = GLA_VAL_WIDTH // GLA_HEADS
GLA_GATE_RANK = 16
GLA_GATE_TEMP = 16.0
GLA_IN = 2 * GLA_KEY_WIDTH + 2 * GLA_VAL_WIDTH + 2 * GLA_GATE_RANK

MOE_GROUPS = 4
MOE_EXPERTS_PER_GROUP = 8
MOE_EXPERTS = MOE_GROUPS * MOE_EXPERTS_PER_GROUP
MOE_TOPK = 2
MOE_HIDDEN = 256

DEEPNORM_ALPHA = (2 * DEPTH) ** 0.25
DEEPNORM_BETA = (8 * DEPTH) ** -0.25
LN_EPS = 1e-5
NORM_EPS = 1e-6
F32 = jnp.float32

kernel_name = "hybrid_bidir_gdn_ssd_hgrn2_gla_hmoe"


def _n_uses(m):
    return (DEPTH - m + N_MIXERS - 1) // N_MIXERS


def _layernorm(t, g, b):
    tf = t.astype(F32)
    mu = jnp.mean(tf, axis=-1, keepdims=True)
    var = jnp.mean(jnp.square(tf - mu), axis=-1, keepdims=True)
    return ((tf - mu) * lax.rsqrt(var + LN_EPS) * g.astype(F32) + b.astype(F32)).astype(t.dtype)


def _rms(t):
    tf = t.astype(F32)
    return tf * lax.rsqrt(jnp.mean(tf * tf, axis=-1, keepdims=True) + NORM_EPS)


def _l2norm(t):
    tf = t.astype(F32)
    return tf * lax.rsqrt(jnp.sum(tf * tf, axis=-1, keepdims=True) + NORM_EPS)


def _dwconv(t, w):
    return lax.conv_general_dilated(
        t, w[:, None, :].astype(t.dtype), window_strides=(1,),
        padding=[(CONV_K // 2, CONV_K // 2)],
        dimension_numbers=("NWC", "WIO", "NWC"),
        feature_group_count=t.shape[-1])


def _rev(t):
    return jnp.flip(t, axis=1)


def _chunks(t):
    bsz, seq = t.shape[:2]
    t = t.reshape(bsz, seq // CHUNK, CHUNK, *t.shape[2:])
    return jnp.swapaxes(jnp.swapaxes(t, 0, 1), 2, 3)


def _unchunks(t):
    t = jnp.swapaxes(jnp.swapaxes(t, 2, 3), 0, 1)
    return t.reshape(t.shape[0], t.shape[1] * t.shape[2], *t.shape[3:])


def _gated_delta_rule(q, k, v, beta, g):
    q, k, v, beta, g = (_chunks(t.astype(F32)) for t in (q, k, v, beta, g))
    gc = jnp.cumsum(g, axis=-1)
    incl = jnp.tril(jnp.ones((CHUNK, CHUNK), dtype=bool))
    decay = jnp.exp(jnp.where(incl, gc[..., :, None] - gc[..., None, :], -jnp.inf))
    eye = jnp.eye(CHUNK, dtype=F32)
    kb = k * beta[..., None]
    lhs = eye + jnp.einsum("nbhik,nbhjk->nbhij", kb, k) * decay * (1.0 - eye)
    solve = functools.partial(lax.linalg.triangular_solve, left_side=True, lower=True,
                              unit_diagonal=True)
    u = solve(lhs, v * beta[..., None])
    w = solve(lhs, kb * jnp.exp(gc)[..., None])
    attn = jnp.einsum("nbhik,nbhjk->nbhij", q, k) * decay
    q_dec = q * jnp.exp(gc)[..., None]
    k_dec = k * jnp.exp(gc[..., -1:] - gc)[..., None]
    g_last = jnp.exp(gc[..., -1])

    def step(state, inp):
        attn_c, q_c, k_c, u_c, w_c, gl = inp
        v_new = u_c - jnp.einsum("bhck,bhkv->bhcv", w_c, state)
        o = jnp.einsum("bhck,bhkv->bhcv", q_c, state) + jnp.einsum("bhij,bhjv->bhiv", attn_c, v_new)
        state = state * gl[..., None, None] + jnp.einsum("bhck,bhcv->bhkv", k_c, v_new)
        return state, o

    s0 = jnp.zeros(q.shape[1:3] + (q.shape[-1], v.shape[-1]), F32)
    _, o = lax.scan(step, s0, (attn, q_dec, k_dec, u, w, g_last))
    return _unchunks(o)


def _ssd_grouped(c, b, u, g):
    c, b, u, g = (_chunks(t.astype(F32)) for t in (c, b, u, g))
    gc = jnp.cumsum(g, axis=3)
    tril = jnp.tril(jnp.ones((CHUNK, CHUNK), dtype=bool))[:, :, None]
    seg = gc[:, :, :, :, None, :] - gc[:, :, :, None, :, :]
    decay = jnp.exp(jnp.where(tril, seg, -jnp.inf))
    scores = jnp.einsum("nbgis,nbgjs->nbgij", c, b)[..., None] * decay
    y_intra = jnp.einsum("nbgijr,nbgjrp->nbgirp", scores, u)

    def step(state, inp):
        c_c, b_c, u_c, gc_c = inp
        y = jnp.exp(gc_c)[..., None] * jnp.einsum("bgis,bgrsp->bgirp", c_c, state)
        g_last = gc_c[:, :, -1]
        u_dec = u_c * jnp.exp(g_last[:, :, None] - gc_c)[..., None]
        state = state * jnp.exp(g_last)[..., None, None] + jnp.einsum("bgjs,bgjrp->bgrsp", b_c, u_dec)
        return state, y

    s0 = jnp.zeros((c.shape[1], c.shape[2], u.shape[4], c.shape[4], u.shape[5]), F32)
    _, y_inter = lax.scan(step, s0, (c, b, u, gc))
    return _unchunks(y_intra + y_inter)


def _gla_chunked(q, k, v, g):
    q, k, v, g = (_chunks(t.astype(F32)) for t in (q, k, v, g))
    gc = jnp.cumsum(g, axis=3)
    tril = jnp.tril(jnp.ones((CHUNK, CHUNK), dtype=bool))[:, :, None]

    def step(state, inp):
        q_c, k_c, v_c, gc_c = inp
        seg = gc_c[:, :, :, None, :] - gc_c[:, :, None, :, :]
        decay = jnp.exp(jnp.where(tril, seg, -jnp.inf))
        attn = jnp.einsum("bhijk,bhjk->bhij", decay * q_c[:, :, :, None, :], k_c)
        g_last = gc_c[:, :, -1]
        o = (jnp.einsum("bhij,bhjv->bhiv", attn, v_c)
             + jnp.einsum("bhik,bhkv->bhiv", q_c * jnp.exp(gc_c), state))
        state = (state * jnp.exp(g_last)[..., None]
                 + jnp.einsum("bhjk,bhjv->bhkv", k_c * jnp.exp(g_last[:, :, None] - gc_c), v_c))
        return state, o

    s0 = jnp.zeros(q.shape[1:3] + (q.shape[-1], v.shape[-1]), F32)
    _, o = lax.scan(step, s0, (q, k, v, gc))
    return _unchunks(o)


def _gdn_mixer(x, w_in, conv_w, a_log, dt_bias, norm_w, w_out):
    bsz, seq, _ = x.shape
    qkv, z, ab = jnp.split(x @ w_in, [GDN_CONV_CH, GDN_CONV_CH + GDN_VAL_WIDTH], axis=-1)
    qkv = jax.nn.silu(_dwconv(qkv, conv_w))
    q, k, v = jnp.split(qkv, [GDN_KEY_WIDTH, 2 * GDN_KEY_WIDTH], axis=-1)
    rep = GDN_V_HEADS // GDN_QK_HEADS
    qk_shape = (bsz, seq, GDN_QK_HEADS, GDN_HEAD_DIM)
    q = jnp.repeat(_l2norm(q.reshape(qk_shape)), rep, axis=2) * GDN_HEAD_DIM ** -0.5
    k = jnp.repeat(_l2norm(k.reshape(qk_shape)), rep, axis=2)
    v = v.reshape(bsz, seq, GDN_V_HEADS, GDN_HEAD_DIM).astype(F32)
    ab = ab.astype(F32).reshape(bsz, seq, 2, 2, GDN_V_HEADS)
    g = -jnp.exp(a_log.astype(F32)) * jax.nn.softplus(ab[:, :, 0] + dt_bias.astype(F32))
    beta = jax.nn.sigmoid(ab[:, :, 1])
    o = (_gated_delta_rule(q, k, v, beta[:, :, 0], g[:, :, 0])
         + _rev(_gated_delta_rule(_rev(q), _rev(k), _rev(v), _rev(beta[:, :, 1]), _rev(g[:, :, 1]))))
    o = _rms(o) * norm_w.astype(F32) * jax.nn.silu(
        z.astype(F32).reshape(bsz, seq, GDN_V_HEADS, GDN_HEAD_DIM))
    return o.reshape(bsz, seq, GDN_VAL_WIDTH).astype(x.dtype) @ w_out


def _mamba2_mixer(x, w_in, conv_w, conv_b, a_log, dt_bias, d_skip, norm_w, w_out):
    bsz, seq, _ = x.shape
    z, xbc, dt = jnp.split(x @ w_in, [M2_D_INNER, M2_D_INNER + M2_CONV_CH], axis=-1)
    xbc = jax.nn.silu(_dwconv(xbc, conv_w) + conv_b).astype(F32)
    xs, b_in, c_out = jnp.split(xbc, [M2_D_INNER, M2_D_INNER + M2_GROUPS * M2_STATE], axis=-1)
    xs = xs.reshape(bsz, seq, M2_GROUPS, M2_HEADS_PER_GROUP, M2_HEAD_DIM)
    b_in = b_in.reshape(bsz, seq, M2_GROUPS, M2_STATE)
    c_out = c_out.reshape(bsz, seq, M2_GROUPS, M2_STATE)
    hshape = (2, M2_GROUPS, M2_HEADS_PER_GROUP)
    dt = jax.nn.softplus(dt.astype(F32).reshape(bsz, seq, *hshape) + dt_bias.astype(F32).reshape(hshape))
    g = -jnp.exp(a_log.astype(F32)).reshape(hshape) * dt
    u_f = xs * dt[:, :, 0, :, :, None]
    u_b = xs * dt[:, :, 1, :, :, None]
    y = (_ssd_grouped(c_out, b_in, u_f, g[:, :, 0])
         + _rev(_ssd_grouped(_rev(c_out), _rev(b_in), _rev(u_b), _rev(g[:, :, 1]))))
    y = y + d_skip.astype(F32).reshape(M2_GROUPS, M2_HEADS_PER_GROUP, 1) * xs
    y = y.reshape(bsz, seq, M2_GROUPS, -1) * jax.nn.silu(z.astype(F32).reshape(bsz, seq, M2_GROUPS, -1))
    y = _rms(y) * norm_w.astype(F32).reshape(M2_GROUPS, -1)
    return y.reshape(bsz, seq, M2_D_INNER).astype(x.dtype) @ w_out


def _hgrn2_mixer(x, w_in, lb, norm_w, w_out):
    bsz, seq, _ = x.shape
    hs = (bsz, seq, HG_HEADS, HG_EXPAND)
    q, f_fwd, f_bwd, i_in, gate = jnp.split((x @ w_in).astype(F32), 5, axis=-1)
    q = jax.nn.silu(q).reshape(hs)
    i_in = i_in.reshape(hs)
    lb = lb.reshape(HG_HEADS, HG_EXPAND)

    def forget(f):
        f = f.reshape(hs)
        log_f = jnp.logaddexp(jnp.log(lb), jnp.log1p(-lb) + jax.nn.log_sigmoid(f))
        return (1.0 - lb) * jax.nn.sigmoid(-f), log_f

    k_f, g_f = forget(f_fwd)
    k_b, g_b = forget(f_bwd)
    o = (_gla_chunked(q, k_f, i_in, g_f)
         + _rev(_gla_chunked(_rev(q), _rev(k_b), _rev(i_in), _rev(g_b))))
    o = _rms(o) * norm_w.astype(F32) * jax.nn.silu(gate.reshape(hs))
    return o.reshape(bsz, seq, HG_WIDTH).astype(x.dtype) @ w_out


def _gla_mixer(x, w_in, w_gk, b_gk, norm_w, w_out):
    bsz, seq, _ = x.shape
    splits = [GLA_KEY_WIDTH, 2 * GLA_KEY_WIDTH, 2 * GLA_KEY_WIDTH + GLA_VAL_WIDTH,
              2 * GLA_KEY_WIDTH + 2 * GLA_VAL_WIDTH]
    q, k, v, gate, r = jnp.split((x @ w_in).astype(F32), splits, axis=-1)
    ks = (bsz, seq, GLA_HEADS, GLA_K)
    vs = (bsz, seq, GLA_HEADS, GLA_V)
    q = q.reshape(ks) * GLA_K ** -0.5
    k = k.reshape(ks)
    v = v.reshape(vs)
    r = r.reshape(bsz, seq, 2, GLA_GATE_RANK)
    gk = jax.nn.log_sigmoid(jnp.einsum("blzr,zrk->blzk", r, w_gk.astype(F32))
                            + b_gk.astype(F32)) / GLA_GATE_TEMP
    g_f = gk[:, :, 0].reshape(ks)
    g_b = gk[:, :, 1].reshape(ks)
    o = (_gla_chunked(q, k, v, g_f)
         + _rev(_gla_chunked(_rev(q), _rev(k), _rev(v), _rev(g_b))))
    o = _rms(o) * norm_w.astype(F32) * jax.nn.silu(gate.reshape(vs))
    return o.reshape(bsz, seq, GLA_VAL_WIDTH).astype(x.dtype) @ w_out


def _hier_moe(x, w_group, b_group, w_router, b_router, w_gate, w_up, w_down):
    bsz, seq, d = x.shape
    xt = x.reshape(bsz * seq, d)
    g_logits = (xt @ w_group).astype(F32) + b_group.astype(F32)
    g_onehot = jax.nn.one_hot(jnp.argmax(g_logits, axis=-1), MOE_GROUPS, dtype=F32)
    g_prob = jnp.sum(jax.nn.softmax(g_logits, axis=-1) * g_onehot, axis=-1)
    e_logits = ((xt @ w_router).astype(F32) + b_router.astype(F32)).reshape(
        -1, MOE_GROUPS, MOE_EXPERTS_PER_GROUP)
    e_logits = jnp.einsum("tge,tg->te", e_logits, g_onehot)
    top_val, top_idx = lax.top_k(e_logits, MOE_TOPK)
    top_w = jax.nn.softmax(top_val, axis=-1) * g_prob[:, None]
    within = jnp.einsum("tk,tke->te", top_w,
                        jax.nn.one_hot(top_idx, MOE_EXPERTS_PER_GROUP, dtype=F32))
    comb = (g_onehot[:, :, None] * within[:, None, :]).astype(x.dtype)
    out = jnp.zeros_like(xt)
    for grp in range(MOE_GROUPS):
        h = (jax.nn.silu(jnp.einsum("td,edf->tef", xt, w_gate[grp]))
             * jnp.einsum("td,edf->tef", xt, w_up[grp]))
        out = out + jnp.einsum("tef,efd->td", h * comb[:, grp, :, None], w_down[grp])
    return out.reshape(bsz, seq, d)


def setup_inputs(seed: int = 0) -> dict:
    key = jax.random.key(seed)
    ks = iter(jax.random.split(key, 48))

    def nrm(shape, scale):
        return jax.random.normal(next(ks), shape, F32) * scale

    def a_log(shape):
        return jnp.log(jax.random.uniform(next(ks), shape, F32, 1.0, 16.0))

    def dt_bias(shape):
        dt = jnp.exp(jax.random.uniform(next(ks), shape, F32, math.log(1e-3), math.log(1e-1)))
        return dt + jnp.log(-jnp.expm1(-dt))

    na, nb, nc, nd = (_n_uses(m) for m in range(N_MIXERS))
    D = D_MODEL
    sd = D ** -0.5
    beta = DEEPNORM_BETA
    return {
        "x": nrm((BATCH, SEQ, D), 1.0),
        "p": nrm((DEPTH, BATCH, SEQ, PLE_DIM), 1.0),
        "gdn_w_in": nrm((na, D, GDN_IN), sd),
        "gdn_conv_w": nrm((na, CONV_K, GDN_CONV_CH), CONV_K ** -0.5),
        "gdn_a_log": a_log((na, 2, GDN_V_HEADS)),
        "gdn_dt_bias": dt_bias((na, 2, GDN_V_HEADS)),
        "gdn_norm_w": 1.0 + nrm((na, GDN_HEAD_DIM), 0.02),
        "gdn_w_out": nrm((na, GDN_VAL_WIDTH, D), GDN_VAL_WIDTH ** -0.5 * beta),
        "m2_w_in": nrm((nb, D, M2_IN), sd),
        "m2_conv_w": nrm((nb, CONV_K, M2_CONV_CH), CONV_K ** -0.5),
        "m2_conv_b": nrm((nb, M2_CONV_CH), 0.02),
        "m2_a_log": a_log((nb, 2, M2_HEADS)),
        "m2_dt_bias": dt_bias((nb, 2, M2_HEADS)),
        "m2_d": 1.0 + nrm((nb, M2_HEADS), 0.1),
        "m2_norm_w": 1.0 + nrm((nb, M2_D_INNER), 0.02),
        "m2_w_out": nrm((nb, M2_D_INNER, D), M2_D_INNER ** -0.5 * beta),
        "hg_w_in": nrm((nc, D, HG_IN), sd),
        "hg_lb_logits": nrm((DEPTH, HG_WIDTH), 0.1),
        "hg_norm_w": 1.0 + nrm((nc, HG_EXPAND), 0.02),
        "hg_w_out": nrm((nc, HG_WIDTH, D), HG_WIDTH ** -0.5 * beta),
        "gla_w_in": nrm((nd, D, GLA_IN), sd),
        "gla_w_gk": nrm((nd, 2, GLA_GATE_RANK, GLA_KEY_WIDTH), GLA_GATE_RANK ** -0.5),
        "gla_b_gk": nrm((nd, 2, GLA_KEY_WIDTH), 0.1),
        "gla_norm_w": 1.0 + nrm((nd, GLA_V), 0.02),
        "gla_w_out": nrm((nd, GLA_VAL_WIDTH, D), GLA_VAL_WIDTH ** -0.5 * beta),
        "ln_g": 1.0 + nrm((DEPTH, 2, D), 0.02),
        "ln_b": nrm((DEPTH, 2, D), 0.02),
        "moe_w_group": nrm((DEPTH, D, MOE_GROUPS), sd),
        "moe_b_group": nrm((DEPTH, MOE_GROUPS), 0.01),
        "moe_w_router": nrm((DEPTH, D, MOE_EXPERTS), sd),
        "moe_b_router": nrm((DEPTH, MOE_EXPERTS), 0.01),
        "moe_w_gate": nrm((DEPTH, MOE_GROUPS, MOE_EXPERTS_PER_GROUP, D, MOE_HIDDEN), sd),
        "moe_w_up": nrm((DEPTH, MOE_GROUPS, MOE_EXPERTS_PER_GROUP, D, MOE_HIDDEN), sd),
        "moe_w_down": nrm((DEPTH, MOE_GROUPS, MOE_EXPERTS_PER_GROUP, MOE_HIDDEN, D),
                          MOE_HIDDEN ** -0.5 * beta),
        "pe_w_gate": nrm((DEPTH, D, D), sd),
        "pe_w_proj": nrm((DEPTH, PLE_DIM, D), PLE_DIM ** -0.5 * beta),
    }


def reference(x, p, gdn_w_in, gdn_conv_w, gdn_a_log, gdn_dt_bias, gdn_norm_w, gdn_w_out,
              m2_w_in, m2_conv_w, m2_conv_b, m2_a_log, m2_dt_bias, m2_d, m2_norm_w, m2_w_out,
              hg_w_in, hg_lb_logits, hg_norm_w, hg_w_out,
              gla_w_in, gla_w_gk, gla_b_gk, gla_norm_w, gla_w_out,
              ln_g, ln_b, moe_w_group, moe_b_group, moe_w_router, moe_b_router,
              moe_w_gate, moe_w_up, moe_w_down, pe_w_gate, pe_w_proj):
    lb_all = jnp.cumsum(jax.nn.softmax(hg_lb_logits.astype(F32), axis=0), axis=0)
    lb_all = lb_all - lb_all[0]
    for i in range(DEPTH):
        m, j = i % N_MIXERS, i // N_MIXERS
        if m == 0:
            h = _gdn_mixer(x, gdn_w_in[j], gdn_conv_w[j], gdn_a_log[j], gdn_dt_bias[j],
                           gdn_norm_w[j], gdn_w_out[j])
        elif m == 1:
            h = _mamba2_mixer(x, m2_w_in[j], m2_conv_w[j], m2_conv_b[j], m2_a_log[j],
                              m2_dt_bias[j], m2_d[j], m2_norm_w[j], m2_w_out[j])
        elif m == 2:
            h = _hgrn2_mixer(x, hg_w_in[j], lb_all[i], hg_norm_w[j], hg_w_out[j])
        else:
            h = _gla_mixer(x, gla_w_in[j], gla_w_gk[j], gla_b_gk[j], gla_norm_w[j], gla_w_out[j])
        x = _layernorm(DEEPNORM_ALPHA * x + h, ln_g[i, 0], ln_b[i, 0])
        h = _hier_moe(x, moe_w_group[i], moe_b_group[i], moe_w_router[i], moe_b_router[i],
                      moe_w_gate[i], moe_w_up[i], moe_w_down[i])
        x = _layernorm(DEEPNORM_ALPHA * x + h, ln_g[i, 1], ln_b[i, 1])
        x = x + jax.nn.sigmoid(x @ pe_w_gate[i]) * (p[i].astype(x.dtype) @ pe_w_proj[i])
    return x
```

```python
import functools
import math

import numpy as np
import jax
import jax.numpy as jnp
from jax import lax
from jax.experimental import pallas as pl
from jax.experimental.pallas import tpu as pltpu

F32 = jnp.float32
BF16 = jnp.bfloat16

CHUNK = 64
CONV_K = 5
LN_EPS = 1e-5
NORM_EPS = 1e-6
NEG_BIG = -1e30
DEPTH = 4
DEEPNORM_ALPHA = (2 * DEPTH) ** 0.25
GLA_GATE_TEMP = 16.0

VMEM_LIMIT_BYTES = 56 * 1024 * 1024

GDN_HEADS = 16
GDN_HEAD_DIM = 128
M2_GROUPS = 4
M2_HEADS_PER_GROUP = 8
M2_HEAD_DIM = 64
M2_STATE = 128
MOE_GROUPS = 4
MOE_EPG = 8
MOE_CAP = 256
MOE_TILE = 1024
ROUTE_ROWS = 16


def _cparams(sem):
    return pltpu.CompilerParams(dimension_semantics=sem, vmem_limit_bytes=VMEM_LIMIT_BYTES)


_NN = (((1,), (0,)), ((), ()))
_NT = (((1,), (1,)), ((), ()))
_TN = (((0,), (0,)), ((), ()))


def _dot(a, b, dims=_NN):
    return lax.dot_general(a.astype(BF16), b.astype(BF16), dims, preferred_element_type=F32)


def _split3(a):
    hi = a.astype(BF16)
    r1 = a - hi.astype(F32)
    mid = r1.astype(BF16)
    lo = (r1 - mid.astype(F32)).astype(BF16)
    return hi, mid, lo


def _dot_sel_l(sel, b, dims=_NN):
    s = sel.astype(BF16)
    out = None
    for part in _split3(b):
        t = lax.dot_general(s, part, dims, preferred_element_type=F32)
        out = t if out is None else out + t
    return out


def _dot_sel_r(a, sel, dims=_NN):
    s = sel.astype(BF16)
    out = None
    for part in _split3(a):
        t = lax.dot_general(part, s, dims, preferred_element_type=F32)
        out = t if out is None else out + t
    return out


def _dot_hi(a, b, dims=_NN):
    ah = a.astype(BF16)
    al = (a - ah.astype(F32)).astype(BF16)
    bh = b.astype(BF16)
    bl = (b - bh.astype(F32)).astype(BF16)
    out = lax.dot_general(ah, bh, dims, preferred_element_type=F32)
    out = out + lax.dot_general(ah, bl, dims, preferred_element_type=F32)
    out = out + lax.dot_general(al, bh, dims, preferred_element_type=F32)
    return out


def _silu(t):
    return t * jax.nn.sigmoid(t)


def _softplus(t):
    return jnp.maximum(t, 0.0) + jnp.log1p(jnp.exp(-jnp.abs(t)))


def _log_sigmoid(t):
    return jnp.minimum(t, 0.0) - jnp.log1p(jnp.exp(-jnp.abs(t)))


def _iota2(shape, axis):
    return lax.broadcasted_iota(jnp.int32, shape, axis)


def _mm_kernel(x_ref, w_ref, o_ref, *, passes):
    if passes == 1:
        o_ref[...] = jnp.dot(x_ref[...].astype(BF16), w_ref[...].astype(BF16),
                             preferred_element_type=F32)
    else:
        o_ref[...] = _dot_hi(x_ref[...], w_ref[...])


def _matmul(x, w, *, passes=1, tm=1024):
    t, k = x.shape
    n = w.shape[1]
    tm = min(tm, t)
    tn = n if n <= 1024 else (1024 if n % 1024 == 0 else 512)
    assert t % tm == 0 and n % tn == 0
    return pl.pallas_call(
        functools.partial(_mm_kernel, passes=passes),
        grid=(t // tm, n // tn),
        in_specs=[pl.BlockSpec((tm, k), lambda i, j: (i, 0)),
                  pl.BlockSpec((k, tn), lambda i, j: (0, j))],
        out_specs=pl.BlockSpec((tm, tn), lambda i, j: (i, j)),
        out_shape=jax.ShapeDtypeStruct((t, n), F32),
        compiler_params=_cparams(("parallel", "parallel")),
        name="proj_matmul",
    )(x, w)


def _conv_kernel(prev_ref, main_ref, next_ref, w_ref, b_ref, o_ref, buf_ref, *,
                 tl, n_norm_blocks, n_scale_blocks, head_dim, scale):
    i = pl.program_id(1)
    j = pl.program_id(2)
    last = pl.num_programs(1) - 1
    keep_prev = jnp.where(i == 0, 0.0, 1.0)
    keep_next = jnp.where(i == last, 0.0, 1.0)
    buf_ref[0:8, :] = prev_ref[0, 0] * keep_prev
    buf_ref[8:8 + tl, :] = main_ref[0]
    buf_ref[8 + tl:16 + tl, :] = next_ref[0, 0] * keep_next
    w = w_ref[...]
    acc = None
    for d in range(CONV_K):
        term = buf_ref[6 + d:6 + d + tl, :] * w[d:d + 1, :]
        acc = term if acc is None else acc + term
    y = _silu(acc + b_ref[...])
    if n_norm_blocks > 0:
        do_norm = j < n_norm_blocks
        fac = jnp.where(j < n_scale_blocks, scale, 1.0)
        pieces = []
        for h in range(y.shape[1] // head_dim):
            yh = y[:, h * head_dim:(h + 1) * head_dim]
            ss = jnp.sum(yh * yh, axis=1, keepdims=True)
            r = jnp.where(do_norm, lax.rsqrt(ss + NORM_EPS) * fac, 1.0)
            pieces.append(yh * r)
        y = jnp.concatenate(pieces, axis=1)
    o_ref[0] = y


def _conv_silu(proj, lane0, width, w, bias, *, n_norm_blocks=0, n_scale_blocks=0,
               head_dim=128, scale=1.0, tc=512):
    b, l, n = proj.shape
    tl = min(512, l)
    assert l % tl == 0 and width % tc == 0 and lane0 % tc == 0 and tl % 8 == 0
    off = lane0 // tc
    halo = proj.reshape(b, l // 8, 8, n)
    nl8 = l // 8
    step8 = tl // 8
    return pl.pallas_call(
        functools.partial(_conv_kernel, tl=tl, n_norm_blocks=n_norm_blocks,
                          n_scale_blocks=n_scale_blocks, head_dim=head_dim, scale=scale),
        grid=(b, l // tl, width // tc),
        in_specs=[
            pl.BlockSpec((1, 1, 8, tc), lambda bi, i, j: (bi, jnp.maximum(i * step8 - 1, 0), 0, off + j)),
            pl.BlockSpec((1, tl, tc), lambda bi, i, j: (bi, i, off + j)),
            pl.BlockSpec((1, 1, 8, tc), lambda bi, i, j: (bi, jnp.minimum((i + 1) * step8, nl8 - 1), 0, off + j)),
            pl.BlockSpec((CONV_K, tc), lambda bi, i, j: (0, j)),
            pl.BlockSpec((1, tc), lambda bi, i, j: (0, j)),
        ],
        out_specs=pl.BlockSpec((1, tl, tc), lambda bi, i, j: (bi, i, j)),
        out_shape=jax.ShapeDtypeStruct((b, l, width), F32),
        scratch_shapes=[pltpu.VMEM((tl + 16, tc), F32)],
        compiler_params=_cparams(("parallel", "parallel", "parallel")),
        name="conv_silu",
    )(halo, proj, halo, w, bias)


GATE_ROWS = 512


def _gate_prep_kernel(raw_ref, bias_ref, alog_ref, tri_l_ref, tri_u_ref, ones_ref,
                      pf_ref, pb_ref, of_ref, ob_ref, *, aux_is_sigmoid):
    raw = raw_ref[...]
    sp = _softplus(raw + bias_ref[...])
    g = -jnp.exp(alog_ref[...]) * sp
    aux = jax.nn.sigmoid(raw) if aux_is_sigmoid else sp
    gc_l = _dot_sel_l(tri_l_ref[...], g)
    gc_u = _dot_sel_l(tri_u_ref[...], g)
    gtot = _dot_sel_l(ones_ref[...], g)
    pf = pf_ref[...]
    pb = pb_ref[...]
    of_ref[...] = (_dot_sel_r(gc_l, pf[0]) + _dot_sel_r(aux, pf[1]) + _dot_sel_r(gtot, pf[2]))
    ob_ref[...] = (_dot_sel_r(gc_u, pb[0]) + _dot_sel_r(aux, pb[1]) + _dot_sel_r(gtot, pb[2]))


def _chunk_masks(rows):
    r = np.arange(rows)
    same = (r[:, None] // CHUNK) == (r[None, :] // CHUNK)
    tri_l = same & (r[None, :] <= r[:, None])
    tri_u = same & (r[None, :] >= r[:, None])
    return (jnp.asarray(tri_l, BF16), jnp.asarray(tri_u, BF16), jnp.asarray(same, BF16))


def _placement(src_gc, src_aux, n_slab, per_slab):
    p = np.zeros((3, 64, n_slab * 128), np.float32)
    for h in range(n_slab * per_slab):
        base = (h // per_slab) * 128 + h % per_slab
        p[0, src_gc[h], base] = 1.0
        p[1, src_aux[h], base + 32] = 1.0
        p[2, src_gc[h], base + 64] = 1.0
    return jnp.asarray(p, BF16)


def _gate_prep(raw, bias64, alog64, pf, pb, *, aux_is_sigmoid):
    t = raw.shape[0]
    rows = min(GATE_ROWS, t)
    w = pf.shape[2]
    tri_l, tri_u, ones = _chunk_masks(rows)
    full = lambda shape: pl.BlockSpec(shape, lambda i: (0,) * len(shape))
    return pl.pallas_call(
        functools.partial(_gate_prep_kernel, aux_is_sigmoid=aux_is_sigmoid),
        grid=(t // rows,),
        in_specs=[pl.BlockSpec((rows, 64), lambda i: (i, 0)),
                  full((1, 64)), full((1, 64)),
                  full((rows, rows)), full((rows, rows)), full((rows, rows)),
                  full((3, 64, w)), full((3, 64, w))],
        out_specs=[pl.BlockSpec((rows, w), lambda i: (i, 0)),
                   pl.BlockSpec((rows, w), lambda i: (i, 0))],
        out_shape=[jax.ShapeDtypeStruct((t, w), F32), jax.ShapeDtypeStruct((t, w), F32)],
        compiler_params=_cparams(("parallel",)),
        name="gate_prep",
    )(raw, bias64, alog64, tri_l, tri_u, ones, pf, pb)


def _row_form(col, b, l):
    w = col.shape[1]
    return jnp.swapaxes(col.reshape(b, l // CHUNK, CHUNK, w), 2, 3)


GDN_HB = 4


def _tri_masks(rev):
    ri = _iota2((CHUNK, CHUNK), 0)
    ci = _iota2((CHUNK, CHUNK), 1)
    if rev:
        return ri, ci, ci >= ri, ci > ri
    return ri, ci, ci <= ri, ci < ri


def _unit_tri_inverse(a, ri, ci, rev):
    eye = jnp.where(ri == ci, 1.0, 0.0)

    def level_mask(s):
        same = (ri >> int(math.log2(2 * s))) == (ci >> int(math.log2(2 * s)))
        r_hi = (ri & (2 * s - 1)) >= s
        c_hi = (ci & (2 * s - 1)) >= s
        if rev:
            return same & (~r_hi) & c_hi
        return same & r_hi & (~c_hi)

    t = eye - jnp.where(level_mask(1), a, 0.0)
    s = 2
    while s < CHUNK:
        a_s = jnp.where(level_mask(s), a, 0.0)
        t = t - _dot_hi(_dot_hi(t, a_s), t)
        s *= 2
    return t


def _gdn_scan_kernel(q_ref, k_ref, v_ref, gcol_ref, grow_ref, o_ref, s_ref, *, hb, rev):
    @pl.when(pl.program_id(2) == 0)
    def _():
        s_ref[...] = jnp.zeros_like(s_ref)

    ri, ci, incl, strict = _tri_masks(rev)
    gcol = gcol_ref[0]
    grow = grow_ref[0, 0]
    d = GDN_HEAD_DIM
    outs = []
    for i in range(hb):
        q = q_ref[0][:, (i // 2) * d:(i // 2 + 1) * d]
        k = k_ref[0][:, (i // 2) * d:(i // 2 + 1) * d]
        v = v_ref[0][:, i * d:(i + 1) * d]
        gc_c = gcol[:, i:i + 1]
        beta_c = gcol[:, 32 + i:33 + i]
        gt_c = gcol[:, 64 + i:65 + i]
        gc_r = grow[i:i + 1, :]
        decay = jnp.exp(jnp.where(incl, gc_c - gc_r, NEG_BIG))
        kk = _dot(k, k, _NT)
        qk = _dot(q, k, _NT)
        a = jnp.where(strict, kk * decay, 0.0) * beta_c
        t = _unit_tri_inverse(a, ri, ci, rev)
        e_gc = jnp.exp(gc_c)
        rhs = jnp.concatenate([v * beta_c, k * (beta_c * e_gc)], axis=1)
        uw = _dot(t, rhs)
        u = uw[:, :d]
        w = uw[:, d:]
        s = s_ref[i]
        wq = _dot(jnp.concatenate([w, q * e_gc], axis=0), s)
        v_new = u - wq[:CHUNK]
        o = wq[CHUNK:] + _dot(qk * decay, v_new)
        k_dec = k * jnp.exp(gt_c - gc_c)
        s_ref[i] = s * jnp.exp(gt_c[0:1, :]) + _dot(k_dec, v_new, _TN)
        outs.append(o)
    o_ref[0] = jnp.concatenate(outs, axis=1)


def _gdn_scan(qkv, col, row, *, rev):
    b, l, _ = qkv.shape
    nc = l // CHUNK
    hb = GDN_HB
    nslab = GDN_HEADS // hb
    cidx = (lambda n: nc - 1 - n) if rev else (lambda n: n)
    return pl.pallas_call(
        functools.partial(_gdn_scan_kernel, hb=hb, rev=rev),
        grid=(b, nslab, nc),
        in_specs=[
            pl.BlockSpec((1, CHUNK, hb * 64), lambda bi, j, n: (bi, cidx(n), j)),
            pl.BlockSpec((1, CHUNK, hb * 64), lambda bi, j, n: (bi, cidx(n), nslab + j)),
            pl.BlockSpec((1, CHUNK, hb * 128), lambda bi, j, n: (bi, cidx(n), nslab + j)),
            pl.BlockSpec((1, CHUNK, 128), lambda bi, j, n: (bi, cidx(n), j)),
            pl.BlockSpec((1, 1, 128, CHUNK), lambda bi, j, n: (bi, cidx(n), j, 0)),
        ],
        out_specs=pl.BlockSpec((1, CHUNK, hb * 128), lambda bi, j, n: (bi, cidx(n), j)),
        out_shape=jax.ShapeDtypeStruct((b, l, GDN_HEADS * GDN_HEAD_DIM), F32),
        scratch_shapes=[pltpu.VMEM((hb, GDN_HEAD_DIM, GDN_HEAD_DIM), F32)],
        compiler_params=_cparams(("parallel", "parallel", "arbitrary")),
        name="gdn_scan_bwd" if rev else "gdn_scan_fwd",
    )(qkv, qkv, qkv, col, row)


def _ssd_scan_kernel(c_ref, b_ref, x_ref, gcol_ref, grow_ref, e8_ref, y_ref, s_ref, *, rev):
    @pl.when(pl.program_id(2) == 0)
    def _():
        s_ref[...] = jnp.zeros_like(s_ref)

    _, _, incl, _ = _tri_masks(rev)
    cc = c_ref[0]
    bb = b_ref[0]
    xs = x_ref[0]
    gcol = gcol_ref[0]
    grow = grow_ref[0, 0]
    e8 = e8_ref[...]
    r_heads = M2_HEADS_PER_GROUP
    gc_x = _dot_sel_r(gcol[:, 0:r_heads], e8)
    dt_x = _dot_sel_r(gcol[:, 32:32 + r_heads], e8)
    gt_x = _dot_sel_r(gcol[:, 64:64 + r_heads], e8)
    u = xs * dt_x
    scores = _dot(cc, bb, _NT)
    s = s_ref[...]
    y = jnp.exp(gc_x) * _dot(cc, s)
    lane_head = _iota2(u.shape, 1) // M2_HEAD_DIM
    for r in range(r_heads):
        decay = jnp.exp(jnp.where(incl, gcol[:, r:r + 1] - grow[r:r + 1, :], NEG_BIG))
        y = y + _dot(scores * decay, jnp.where(lane_head == r, u, 0.0))
    s_ref[...] = s * jnp.exp(gt_x[0:1, :]) + _dot(bb, u * jnp.exp(gt_x - gc_x), _TN)
    y_ref[0] = y


def _ssd_scan(xbc, col, row, e8, *, rev):
    b, l, _ = xbc.shape
    nc = l // CHUNK
    gw = M2_HEADS_PER_GROUP * M2_HEAD_DIM
    xs_blocks = M2_GROUPS * gw // M2_STATE
    cidx = (lambda n: nc - 1 - n) if rev else (lambda n: n)
    return pl.pallas_call(
        functools.partial(_ssd_scan_kernel, rev=rev),
        grid=(b, M2_GROUPS, nc),
        in_specs=[
            pl.BlockSpec((1, CHUNK, M2_STATE), lambda bi, g, n: (bi, cidx(n), xs_blocks + M2_GROUPS + g)),
            pl.BlockSpec((1, CHUNK, M2_STATE), lambda bi, g, n: (bi, cidx(n), xs_blocks + g)),
            pl.BlockSpec((1, CHUNK, gw), lambda bi, g, n: (bi, cidx(n), g)),
            pl.BlockSpec((1, CHUNK, 128), lambda bi, g, n: (bi, cidx(n), g)),
            pl.BlockSpec((1, 1, 128, CHUNK), lambda bi, g, n: (bi, cidx(n), g, 0)),
            pl.BlockSpec((M2_HEADS_PER_GROUP, gw), lambda bi, g, n: (0, 0)),
        ],
        out_specs=pl.BlockSpec((1, CHUNK, gw), lambda bi, g, n: (bi, cidx(n), g)),
        out_shape=jax.ShapeDtypeStruct((b, l, M2_GROUPS * gw), F32),
        scratch_shapes=[pltpu.VMEM((M2_STATE, gw), F32)],
        compiler_params=_cparams(("parallel", "parallel", "arbitrary")),
        name="ssd_scan_bwd" if rev else "ssd_scan_fwd",
    )(xbc, xbc, xbc, col, row, e8)


def _gla_chunk(q, k, v, g, st, rev):
    c = CHUNK
    ri, ci, incl, _ = _tri_masks(rev)
    gsum = _dot_sel_l(jnp.where(incl, 1.0, 0.0), g)
    gtot = gsum[0:1] if rev else gsum[c - 1:c]
    o = _dot(q * jnp.exp(gsum), st, _NT)
    st_new = st * jnp.exp(gtot) + _dot(v, k * jnp.exp(gtot - gsum), _TN)

    row = _iota2((c, 1), 0)
    attn = jnp.zeros((c, c), F32)
    s = c // 2
    while s >= 8:
        pieces = []
        for p in range(c // (2 * s)):
            r = p * 2 * s + (s if rev else s - 1)
            pieces.append(jnp.broadcast_to(gsum[r:r + 1], (2 * s, gsum.shape[1])))
        ref = pieces[0] if len(pieces) == 1 else jnp.concatenate(pieces, axis=0)
        e = jnp.exp(-jnp.abs(gsum - ref))
        upper_half = (row & (2 * s - 1)) >= s
        q_rows = (~upper_half) if rev else upper_half
        qs = jnp.where(q_rows, q * e, 0.0)
        ks = jnp.where(q_rows, 0.0, k * e)
        shift = int(math.log2(2 * s))
        same_pair = (ri >> shift) == (ci >> shift)
        attn = attn + jnp.where(same_pair, _dot(qs, ks, _NT), 0.0)
        s //= 2

    lane = _iota2((8, c), 1)
    sub = _iota2((8, 1), 0)
    blocks = []
    for blk in range(c // 8):
        qb = q[8 * blk:8 * blk + 8]
        kb = k[8 * blk:8 * blk + 8]
        gb = gsum[8 * blk:8 * blk + 8]
        acc = jnp.zeros((8, c), F32)
        for j in range(8):
            e = jnp.exp(jnp.minimum(gb - gb[j:j + 1], 0.0))
            red = jnp.sum(qb * kb[j:j + 1] * e, axis=1, keepdims=True)
            ok = (sub <= j) if rev else (sub >= j)
            acc = jnp.where((lane == 8 * blk + j) & ok, red, acc)
        blocks.append(acc)
    attn = attn + jnp.concatenate(blocks, axis=0)
    o = o + _dot(attn, v)
    return o, st_new


HG_HB = 2
HG_HEADS = 8
HG_DIM = 128


def _hg_scan_kernel(q_ref, f_ref, v_ref, lb_ref, o_ref, st_ref, *, hb, rev):
    @pl.when(pl.program_id(2) == 0)
    def _():
        st_ref[...] = jnp.zeros_like(st_ref)

    lbp = lb_ref[...]
    d = HG_DIM
    outs = []
    for i in range(hb):
        sl = slice(i * d, (i + 1) * d)
        q = _silu(q_ref[0][:, sl])
        f = f_ref[0][:, sl]
        k = lbp[0:1, sl] * jax.nn.sigmoid(-f)
        a = lbp[1:2, sl]
        bterm = lbp[2:3, sl] + _log_sigmoid(f)
        g = jnp.maximum(a, bterm) + jnp.log1p(jnp.exp(-jnp.abs(a - bterm)))
        o, st_new = _gla_chunk(q, k, v_ref[0][:, sl], g, st_ref[i], rev)
        st_ref[i] = st_new
        outs.append(o)
    o_ref[0] = jnp.concatenate(outs, axis=1)


def _hg_scan(proj, lbp, *, rev):
    b, l, _ = proj.shape
    nc = l // CHUNK
    hb = HG_HB
    nsl = HG_HEADS // hb
    f_off = (2 if rev else 1) * nsl
    cidx = (lambda n: nc - 1 - n) if rev else (lambda n: n)
    blk = (1, CHUNK, hb * HG_DIM)
    return pl.pallas_call(
        functools.partial(_hg_scan_kernel, hb=hb, rev=rev),
        grid=(b, nsl, nc),
        in_specs=[
            pl.BlockSpec(blk, lambda bi, j, n: (bi, cidx(n), j)),
            pl.BlockSpec(blk, lambda bi, j, n: (bi, cidx(n), f_off + j)),
            pl.BlockSpec(blk, lambda bi, j, n: (bi, cidx(n), 3 * nsl + j)),
            pl.BlockSpec((8, hb * HG_DIM), lambda bi, j, n: (0, j)),
        ],
        out_specs=pl.BlockSpec(blk, lambda bi, j, n: (bi, cidx(n), j)),
        out_shape=jax.ShapeDtypeStruct((b, l, HG_HEADS * HG_DIM), F32),
        scratch_shapes=[pltpu.VMEM((hb, HG_DIM, HG_DIM), F32)],
        compiler_params=_cparams(("parallel", "parallel", "arbitrary")),
        name="hgrn2_scan_bwd" if rev else "hgrn2_scan_fwd",
    )(proj, proj, proj, lbp)


GLA_HB = 2
GLA_HEADS = 4
GLA_K = 128
GLA_V = 256


def _gla_scan_kernel(q_ref, k_ref, v_ref, r_ref, wgk_ref, bgk_ref, o_ref, st_ref, *, hb, rev):
    @pl.when(pl.program_id(2) == 0)
    def _():
        st_ref[...] = jnp.zeros_like(st_ref)

    pre = _dot_hi(r_ref[0], wgk_ref[...]) + bgk_ref[...]
    gk = _log_sigmoid(pre) * (1.0 / GLA_GATE_TEMP)
    outs = []
    for i in range(hb):
        sl = slice(i * GLA_K, (i + 1) * GLA_K)
        q = q_ref[0][:, sl] * (GLA_K ** -0.5)
        o, st_new = _gla_chunk(q, k_ref[0][:, sl], v_ref[0][:, i * GLA_V:(i + 1) * GLA_V],
                               gk[:, sl], st_ref[i], rev)
        st_ref[i] = st_new
        outs.append(o)
    o_ref[0] = jnp.concatenate(outs, axis=1)


def _gla_scan(proj, r, wgk, bgk, *, rev):
    b, l, _ = proj.shape
    nc = l // CHUNK
    hb = GLA_HB
    nsl = GLA_HEADS // hb
    cidx = (lambda n: nc - 1 - n) if rev else (lambda n: n)
    return pl.pallas_call(
        functools.partial(_gla_scan_kernel, hb=hb, rev=rev),
        grid=(b, nsl, nc),
        in_specs=[
            pl.BlockSpec((1, CHUNK, hb * GLA_K), lambda bi, j, n: (bi, cidx(n), j)),
            pl.BlockSpec((1, CHUNK, hb * GLA_K), lambda bi, j, n: (bi, cidx(n), nsl + j)),
            pl.BlockSpec((1, CHUNK, hb * GLA_V), lambda bi, j, n: (bi, cidx(n), nsl + j)),
            pl.BlockSpec((1, CHUNK, r.shape[2]), lambda bi, j, n: (bi, cidx(n), 0)),
            pl.BlockSpec((wgk.shape[0], hb * GLA_K), lambda bi, j, n: (0, j)),
            pl.BlockSpec((1, hb * GLA_K), lambda bi, j, n: (0, j)),
        ],
        out_specs=pl.BlockSpec((1, CHUNK, hb * GLA_V), lambda bi, j, n: (bi, cidx(n), j)),
        out_shape=jax.ShapeDtypeStruct((b, l, GLA_HEADS * GLA_V), F32),
        scratch_shapes=[pltpu.VMEM((hb, GLA_V, GLA_K), F32)],
        compiler_params=_cparams(("parallel", "parallel", "arbitrary")),
        name="gla_scan_bwd" if rev else "gla_scan_fwd",
    )(proj, proj, proj, r, wgk, bgk)


def _layernorm_rows(t, g, b):
    mu = jnp.mean(t, axis=1, keepdims=True)
    tc = t - mu
    var = jnp.mean(tc * tc, axis=1, keepdims=True)
    return tc * lax.rsqrt(var + LN_EPS) * g + b


def _out_ln_kernel(*refs, group, gate_first, has_skip):
    if has_skip:
        of_ref, ob_ref, z_ref, xs_ref, d_ref, nw_ref, w_ref, x_ref, g_ref, b_ref, o_ref = refs
        y = of_ref[...] + ob_ref[...] + d_ref[...] * xs_ref[...]
    else:
        of_ref, ob_ref, z_ref, nw_ref, w_ref, x_ref, g_ref, b_ref, o_ref = refs
        y = of_ref[...] + ob_ref[...]
    gate = _silu(z_ref[...])
    if gate_first:
        y = y * gate
    pieces = []
    for h in range(y.shape[1] // group):
        yh = y[:, h * group:(h + 1) * group]
        ms = jnp.mean(yh * yh, axis=1, keepdims=True)
        pieces.append(yh * lax.rsqrt(ms + NORM_EPS))
    y = jnp.concatenate(pieces, axis=1) * nw_ref[...]
    if not gate_first:
        y = y * gate
    h_out = _dot(y, w_ref[...])
    o_ref[...] = _layernorm_rows(DEEPNORM_ALPHA * x_ref[...] + h_out, g_ref[...], b_ref[...])


def _out_ln(o_f, o_b, z_arr, z_block, nw_full, w_out, x_res, ln_g, ln_b, *, group, gate_first,
            skip=None, tm=256):
    t, din = o_f.shape
    d = w_out.shape[1]
    tm = min(tm, t)
    row = lambda width, blk=0: pl.BlockSpec((tm, width), lambda i: (i, blk))
    full = lambda shape: pl.BlockSpec(shape, lambda i: (0,) * len(shape))
    ins = [o_f, o_b, z_arr]
    specs = [row(din), row(din), row(din, z_block)]
    if skip is not None:
        xs_arr, d_full = skip
        ins += [xs_arr, d_full]
        specs += [row(din), full((1, din))]
    ins += [nw_full, w_out, x_res, ln_g, ln_b]
    specs += [full((1, din)), full((din, d)), row(d), full((1, d)), full((1, d))]
    return pl.pallas_call(
        functools.partial(_out_ln_kernel, group=group, gate_first=gate_first, has_skip=skip is not None),
        grid=(t // tm,),
        in_specs=specs,
        out_specs=row(d),
        out_shape=jax.ShapeDtypeStruct((t, d), F32),
        compiler_params=_cparams(("parallel",)),
        name="gated_norm_out_ln",
    )(*ins)


def _router_kernel(x_ref, wt_ref, bias_ref, tri_ref, row_ref, col_ref, cnt_ref):
    x = x_ref[...]
    tm = x.shape[0]
    logits = _dot_hi(wt_ref[...], x, _NT) + bias_ref[...]
    sub8 = _iota2((8, tm), 0).astype(F32)
    gl = jnp.where(sub8 < MOE_GROUPS, logits[0:8], NEG_BIG)
    gmax = jnp.max(gl, axis=0, keepdims=True)
    gidx = jnp.min(jnp.where(gl == gmax, sub8, 8.0), axis=0, keepdims=True)
    gprob = 1.0 / jnp.sum(jnp.exp(gl - gmax), axis=0, keepdims=True)
    el = logits[8:16]
    for g in range(1, MOE_GROUPS):
        el = jnp.where(gidx == g, logits[8 + 8 * g:16 + 8 * g], el)
    m1 = jnp.max(el, axis=0, keepdims=True)
    i1 = jnp.min(jnp.where(el == m1, sub8, 8.0), axis=0, keepdims=True)
    el2 = jnp.where(sub8 == i1, NEG_BIG, el)
    m2 = jnp.max(el2, axis=0, keepdims=True)
    i2 = jnp.min(jnp.where(el2 == m2, sub8, 8.0), axis=0, keepdims=True)
    e2 = jnp.exp(m2 - m1)
    w1 = gprob / (1.0 + e2)
    w2 = gprob * e2 / (1.0 + e2)
    within = jnp.where(sub8 == i1, w1, 0.0) + jnp.where(sub8 == i2, w2, 0.0)
    onehot = jnp.where(sub8 == gidx, 1.0, 0.0)
    before = _dot(onehot, tri_ref[...])
    pos = jnp.sum(onehot * before, axis=0, keepdims=True)
    rec = jnp.where(sub8 == 0, gidx, jnp.where(sub8 == 1, pos, 0.0))
    rows = jnp.concatenate([within, rec], axis=0)
    row_ref[...] = rows
    padded = jnp.concatenate([rows, jnp.zeros((128 - ROUTE_ROWS, tm), F32)], axis=0)
    col_ref[...] = padded.T
    cnt_ref[0] = jnp.broadcast_to(jnp.sum(onehot, axis=1, keepdims=True), (8, 128))


def _router(x, wt, bias_col):
    t, d = x.shape
    tm = min(MOE_TILE, t)
    nt = t // tm
    r = np.arange(tm)
    tri = jnp.asarray(r[:, None] < r[None, :], BF16)
    full = lambda shape: pl.BlockSpec(shape, lambda i: (0,) * len(shape))
    return pl.pallas_call(
        _router_kernel,
        grid=(nt,),
        in_specs=[pl.BlockSpec((tm, d), lambda i: (i, 0)), full((128, d)), full((128, 1)), full((tm, tm))],
        out_specs=[pl.BlockSpec((ROUTE_ROWS, tm), lambda i: (0, i)),
                   pl.BlockSpec((tm, 128), lambda i: (i, 0)),
                   pl.BlockSpec((1, 8, 128), lambda i: (i, 0, 0))],
        out_shape=[jax.ShapeDtypeStruct((ROUTE_ROWS, t), F32),
                   jax.ShapeDtypeStruct((t, 128), F32),
                   jax.ShapeDtypeStruct((nt, 8, 128), F32)],
        compiler_params=_cparams(("parallel",)),
        name="moe_router",
    )(x, wt, bias_col, tri)


def _moe_kernel(cnt_ref, x_ref, row_ref, col_ref, wg_ref, wu_ref, wd_ref, o_ref, *, cap):
    g = pl.program_id(0)
    ti = pl.program_id(1)
    n = cnt_ref[ti * MOE_GROUPS + g]
    nb = (n + cap - 1) // cap
    tm = x_ref.shape[0]
    gf = g.astype(F32)
    rows = row_ref[...]
    cols = col_ref[...]
    in_g_row = rows[8:9, :] == gf
    pos_row = rows[9:10, :]
    in_g_col = cols[:, 8:9] == gf
    pos_col = cols[:, 9:10]
    xb = x_ref[...].astype(BF16)
    o_ref[0] = jnp.zeros((tm, o_ref.shape[2]), F32)

    def body(bi, carry):
        base = (bi * cap).astype(F32)
        slot_r = _iota2((cap, tm), 0).astype(F32) + base
        sel = jnp.where(in_g_row & (slot_r == pos_row), 1.0, 0.0).astype(BF16)
        xg = jnp.dot(sel, xb, preferred_element_type=F32).astype(BF16)
        cw = _dot_sel_l(sel, cols)
        y = jnp.zeros((cap, o_ref.shape[2]), F32)
        for e in range(MOE_EPG):
            hg = jnp.dot(xg, wg_ref[0, e], preferred_element_type=F32)
            hu = jnp.dot(xg, wu_ref[0, e], preferred_element_type=F32)
            h = _silu(hg) * hu * cw[:, e:e + 1]
            y = y + _dot(h, wd_ref[0, e])
        slot_c = _iota2((tm, cap), 1).astype(F32) + base
        sel_t = jnp.where(in_g_col & (slot_c == pos_col), 1.0, 0.0).astype(BF16)
        yh = y.astype(BF16)
        yl = (y - yh.astype(F32)).astype(BF16)
        o_ref[0] += (jnp.dot(sel_t, yh, preferred_element_type=F32)
                     + jnp.dot(sel_t, yl, preferred_element_type=F32))
        return carry

    lax.fori_loop(0, nb, body, 0)


def _moe(x, route_row, route_col, counts, wg, wu, wd):
    t, d = x.shape
    tm = min(MOE_TILE, t)
    cap = min(MOE_CAP, tm)
    nt = t // tm
    hid = wg.shape[3]
    grid_spec = pltpu.PrefetchScalarGridSpec(
        num_scalar_prefetch=1,
        grid=(MOE_GROUPS, nt),
        in_specs=[
            pl.BlockSpec((tm, d), lambda g, i, c: (i, 0)),
            pl.BlockSpec((ROUTE_ROWS, tm), lambda g, i, c: (0, i)),
            pl.BlockSpec((tm, 128), lambda g, i, c: (i, 0)),
            pl.BlockSpec((1, MOE_EPG, d, hid), lambda g, i, c: (g, 0, 0, 0)),
            pl.BlockSpec((1, MOE_EPG, d, hid), lambda g, i, c: (g, 0, 0, 0)),
            pl.BlockSpec((1, MOE_EPG, hid, d), lambda g, i, c: (g, 0, 0, 0)),
        ],
        out_specs=pl.BlockSpec((1, tm, d), lambda g, i, c: (g, i, 0)),
    )
    return pl.pallas_call(
        functools.partial(_moe_kernel, cap=cap),
        grid_spec=grid_spec,
        out_shape=jax.ShapeDtypeStruct((MOE_GROUPS, t, d), F32),
        compiler_params=_cparams(("arbitrary", "arbitrary")),
        name="moe_experts",
    )(counts, x, route_row, route_col, wg, wu, wd)


def _ln_pe_kernel(x_ref, m_ref, p_ref, g_ref, b_ref, wg_ref, wp_ref, o_ref):
    h = m_ref[0]
    for g in range(1, MOE_GROUPS):
        h = h + m_ref[g]
    y = _layernorm_rows(DEEPNORM_ALPHA * x_ref[...] + h, g_ref[...], b_ref[...])
    gate = jax.nn.sigmoid(_dot(y, wg_ref[...]))
    o_ref[...] = y + gate * _dot(p_ref[...], wp_ref[...])


def _ln_pe(x, moe_parts, p, ln_g, ln_b, w_gate, w_proj, *, tm=512):
    t, d = x.shape
    pd = p.shape[1]
    tm = min(tm, t)
    full = lambda shape: pl.BlockSpec(shape, lambda i: (0,) * len(shape))
    return pl.pallas_call(
        _ln_pe_kernel,
        grid=(t // tm,),
        in_specs=[pl.BlockSpec((tm, d), lambda i: (i, 0)),
                  pl.BlockSpec((MOE_GROUPS, tm, d), lambda i: (0, i, 0)),
                  pl.BlockSpec((tm, pd), lambda i: (i, 0)),
                  full((1, d)), full((1, d)), full((d, d)), full((pd, d))],
        out_specs=pl.BlockSpec((tm, d), lambda i: (i, 0)),
        out_shape=jax.ShapeDtypeStruct((t, d), F32),
        compiler_params=_cparams(("parallel",)),
        name="ln_input_embedding",
    )(x, moe_parts, p, ln_g, ln_b, w_gate, w_proj)


def _pad64(v):
    v = v.reshape(1, -1).astype(F32)
    return jnp.pad(v, ((0, 0), (0, 64 - v.shape[1])))


def _gdn_mixer(x2, b, l, w_in, conv_w, a_log, dt_bias, norm_w, w_out, x_res, ln_g, ln_b):
    t = b * l
    conv_ch = 4096
    main = _matmul(x2, w_in[:, :6144].astype(BF16))
    ab = _matmul(x2, w_in[:, 6144:], passes=3)
    qkv = _conv_silu(main.reshape(b, l, 6144), 0, conv_ch, conv_w.astype(F32),
                     jnp.zeros((1, conv_ch), F32), n_norm_blocks=4, n_scale_blocks=2,
                     head_dim=GDN_HEAD_DIM, scale=GDN_HEAD_DIM ** -0.5)
    per_slab = GDN_HB
    nslab = GDN_HEADS // per_slab
    heads = np.arange(GDN_HEADS)
    pf = _placement(heads, 32 + heads, nslab, per_slab)
    pb = _placement(16 + heads, 48 + heads, nslab, per_slab)
    col_f, col_b = _gate_prep(ab, _pad64(dt_bias), _pad64(a_log), pf, pb, aux_is_sigmoid=True)
    o_f = _gdn_scan(qkv, col_f.reshape(b, l, -1), _row_form(col_f, b, l), rev=False)
    o_b = _gdn_scan(qkv, col_b.reshape(b, l, -1), _row_form(col_b, b, l), rev=True)
    nw = jnp.tile(norm_w.astype(F32), GDN_HEADS).reshape(1, -1)
    return _out_ln(o_f.reshape(t, -1), o_b.reshape(t, -1), main, 2, nw, w_out.astype(BF16),
                   x_res, ln_g, ln_b, group=GDN_HEAD_DIM, gate_first=False)


def _mamba2_mixer(x2, b, l, w_in, conv_w, conv_b, a_log, dt_bias, d_skip, norm_w, w_out,
                  x_res, ln_g, ln_b):
    t = b * l
    d_inner = 2048
    conv_ch = 3072
    main = _matmul(x2, w_in[:, :d_inner + conv_ch].astype(BF16))
    dt = _matmul(x2, w_in[:, d_inner + conv_ch:], passes=3)
    xbc = _conv_silu(main.reshape(b, l, -1), d_inner, conv_ch, conv_w.astype(F32),
                     conv_b.reshape(1, -1).astype(F32))
    heads = np.arange(M2_GROUPS * M2_HEADS_PER_GROUP)
    pf = _placement(heads, heads, M2_GROUPS, M2_HEADS_PER_GROUP)
    pb = _placement(32 + heads, 32 + heads, M2_GROUPS, M2_HEADS_PER_GROUP)
    col_f, col_b = _gate_prep(dt, _pad64(dt_bias), _pad64(a_log), pf, pb, aux_is_sigmoid=False)
    lanes = np.arange(M2_HEADS_PER_GROUP * M2_HEAD_DIM)
    e8 = jnp.asarray(np.arange(M2_HEADS_PER_GROUP)[:, None] == lanes[None, :] // M2_HEAD_DIM, BF16)
    y_f = _ssd_scan(xbc, col_f.reshape(b, l, -1), _row_form(col_f, b, l), e8, rev=False)
    y_b = _ssd_scan(xbc, col_b.reshape(b, l, -1), _row_form(col_b, b, l), e8, rev=True)
    d_full = jnp.repeat(d_skip.astype(F32), M2_HEAD_DIM).reshape(1, -1)
    return _out_ln(y_f.reshape(t, -1), y_b.reshape(t, -1), main, 0,
                   norm_w.reshape(1, -1).astype(F32), w_out.astype(BF16), x_res, ln_g, ln_b,
                   group=M2_HEADS_PER_GROUP * M2_HEAD_DIM, gate_first=True,
                   skip=(xbc.reshape(t, -1), d_full))


def _hgrn2_mixer(x2, b, l, w_in, lb, norm_w, w_out, x_res, ln_g, ln_b):
    t = b * l
    proj = _matmul(x2, w_in.astype(BF16))
    lb = lb.reshape(1, -1).astype(F32)
    lbp = jnp.concatenate([1.0 - lb, jnp.log(lb), jnp.log1p(-lb), jnp.zeros((5, lb.shape[1]), F32)], axis=0)
    p3 = proj.reshape(b, l, -1)
    o_f = _hg_scan(p3, lbp, rev=False)
    o_b = _hg_scan(p3, lbp, rev=True)
    nw = jnp.tile(norm_w.astype(F32), HG_HEADS).reshape(1, -1)
    return _out_ln(o_f.reshape(t, -1), o_b.reshape(t, -1), proj, 4, nw, w_out.astype(BF16),
                   x_res, ln_g, ln_b, group=HG_DIM, gate_first=False)


def _gla_mixer(x2, b, l, w_in, w_gk, b_gk, norm_w, w_out, x_res, ln_g, ln_b):
    t = b * l
    main = _matmul(x2, w_in[:, :3072].astype(BF16))
    r = _matmul(x2, w_in[:, 3072:], passes=3)
    p3 = main.reshape(b, l, -1)
    r3 = r.reshape(b, l, -1)
    rank = w_gk.shape[1]
    zeros = jnp.zeros_like(w_gk[0])
    w_f = jnp.concatenate([w_gk[0], zeros], axis=0).astype(F32)
    w_b = jnp.concatenate([zeros, w_gk[1]], axis=0).astype(F32)
    del rank
    o_f = _gla_scan(p3, r3, w_f, b_gk[0].reshape(1, -1).astype(F32), rev=False)
    o_b = _gla_scan(p3, r3, w_b, b_gk[1].reshape(1, -1).astype(F32), rev=True)
    nw = jnp.tile(norm_w.astype(F32), GLA_HEADS).reshape(1, -1)
    return _out_ln(o_f.reshape(t, -1), o_b.reshape(t, -1), main, 2, nw, w_out.astype(BF16),
                   x_res, ln_g, ln_b, group=GLA_V, gate_first=False)


def _hier_moe_parts(x1, w_group, b_group, w_router, b_router, w_gate, w_up, w_down):
    d = x1.shape[1]
    wt = jnp.zeros((128, d), F32)
    wt = wt.at[0:MOE_GROUPS].set(w_group.T.astype(F32))
    wt = wt.at[8:8 + MOE_GROUPS * MOE_EPG].set(w_router.T.astype(F32))
    bias = jnp.zeros((128, 1), F32)
    bias = bias.at[0:MOE_GROUPS, 0].set(b_group.astype(F32))
    bias = bias.at[8:8 + MOE_GROUPS * MOE_EPG, 0].set(b_router.astype(F32))
    route_row, route_col, cnt = _router(x1, wt, bias)
    counts = cnt[:, :MOE_GROUPS, 0].astype(jnp.int32).reshape(-1)
    return _moe(x1, route_row, route_col, counts,
                w_gate.astype(BF16), w_up.astype(BF16), w_down.astype(BF16))


def kernel(x, p, gdn_w_in, gdn_conv_w, gdn_a_log, gdn_dt_bias, gdn_norm_w, gdn_w_out, m2_w_in, m2_conv_w, m2_conv_b, m2_a_log, m2_dt_bias, m2_d, m2_norm_w, m2_w_out, hg_w_in, hg_lb_logits, hg_norm_w, hg_w_out, gla_w_in, gla_w_gk, gla_b_gk, gla_norm_w, gla_w_out, ln_g, ln_b, moe_w_group, moe_b_group, moe_w_router, moe_b_router, moe_w_gate, moe_w_up, moe_w_down, pe_w_gate, pe_w_proj):
    b, l, d = x.shape
    t = b * l
    depth = p.shape[0]
    lb_all = jnp.cumsum(jax.nn.softmax(hg_lb_logits.astype(F32), axis=0), axis=0)
    lb_all = lb_all - lb_all[0]
    xc = x.reshape(t, d).astype(F32)
    for i in range(depth):
        m, j = i % 4, i // 4
        g1 = ln_g[i, 0].reshape(1, d).astype(F32)
        b1 = ln_b[i, 0].reshape(1, d).astype(F32)
        if m == 0:
            x1 = _gdn_mixer(xc, b, l, gdn_w_in[j], gdn_conv_w[j], gdn_a_log[j], gdn_dt_bias[j],
                            gdn_norm_w[j], gdn_w_out[j], xc, g1, b1)
        elif m == 1:
            x1 = _mamba2_mixer(xc, b, l, m2_w_in[j], m2_conv_w[j], m2_conv_b[j], m2_a_log[j],
                               m2_dt_bias[j], m2_d[j], m2_norm_w[j], m2_w_out[j], xc, g1, b1)
        elif m == 2:
            x1 = _hgrn2_mixer(xc, b, l, hg_w_in[j], lb_all[i], hg_norm_w[j], hg_w_out[j], xc, g1, b1)
        else:
            x1 = _gla_mixer(xc, b, l, gla_w_in[j], gla_w_gk[j], gla_b_gk[j], gla_norm_w[j],
                            gla_w_out[j], xc, g1, b1)
        parts = _hier_moe_parts(x1, moe_w_group[i], moe_b_group[i], moe_w_router[i], moe_b_router[i],
                                moe_w_gate[i], moe_w_up[i], moe_w_down[i])
        xc = _ln_pe(x1, parts, p[i].reshape(t, -1).astype(F32),
                    ln_g[i, 1].reshape(1, d).astype(F32), ln_b[i, 1].reshape(1, d).astype(F32),
                    pe_w_gate[i].astype(BF16), pe_w_proj[i].astype(BF16))
    return xc.reshape(b, l, d).astype(x.dtype)
```

```python
import functools
import math

import numpy as np
import jax
import jax.numpy as jnp
from jax import lax
from jax.experimental import pallas as pl
from jax.experimental.pallas import tpu as pltpu

F32 = jnp.float32
BF16 = jnp.bfloat16

CHUNK = 64
CONV_K = 5
LN_EPS = 1e-5
NORM_EPS = 1e-6
NEG_BIG = -1e30
DEPTH = 4
DEEPNORM_ALPHA = (2 * DEPTH) ** 0.25
GLA_GATE_TEMP = 16.0

VMEM_LIMIT_BYTES = 56 * 1024 * 1024

GDN_HEADS = 16
GDN_HEAD_DIM = 128
M2_GROUPS = 4
M2_HEADS_PER_GROUP = 8
M2_HEAD_DIM = 64
M2_STATE = 128
MOE_GROUPS = 4
MOE_EPG = 8
MOE_CAP = 320
MOE_TILE = 1024
ROUTE_ROWS = 16


def _cparams(sem):
    return pltpu.CompilerParams(dimension_semantics=sem, vmem_limit_bytes=VMEM_LIMIT_BYTES)


_NN = (((1,), (0,)), ((), ()))
_NT = (((1,), (1,)), ((), ()))
_TN = (((0,), (0,)), ((), ()))


def _dot(a, b, dims=_NN):
    return lax.dot_general(a.astype(BF16), b.astype(BF16), dims, preferred_element_type=F32)


def _split3(a):
    hi = a.astype(BF16)
    r1 = a - hi.astype(F32)
    mid = r1.astype(BF16)
    lo = (r1 - mid.astype(F32)).astype(BF16)
    return hi, mid, lo


def _dot_sel_l(sel, b, dims=_NN):
    s = sel.astype(BF16)
    out = None
    for part in _split3(b):
        t = lax.dot_general(s, part, dims, preferred_element_type=F32)
        out = t if out is None else out + t
    return out


def _dot_sel_r(a, sel, dims=_NN):
    s = sel.astype(BF16)
    out = None
    for part in _split3(a):
        t = lax.dot_general(part, s, dims, preferred_element_type=F32)
        out = t if out is None else out + t
    return out


def _dot_hi(a, b, dims=_NN):
    ah = a.astype(BF16)
    al = (a - ah.astype(F32)).astype(BF16)
    bh = b.astype(BF16)
    bl = (b - bh.astype(F32)).astype(BF16)
    out = lax.dot_general(ah, bh, dims, preferred_element_type=F32)
    out = out + lax.dot_general(ah, bl, dims, preferred_element_type=F32)
    out = out + lax.dot_general(al, bh, dims, preferred_element_type=F32)
    return out


def _silu(t):
    return t * jax.nn.sigmoid(t)


def _softplus(t):
    return jnp.maximum(t, 0.0) + jnp.log1p(jnp.exp(-jnp.abs(t)))


def _log_sigmoid(t):
    return jnp.minimum(t, 0.0) - jnp.log1p(jnp.exp(-jnp.abs(t)))


def _iota2(shape, axis):
    return lax.broadcasted_iota(jnp.int32, shape, axis)


def _mm_kernel(x_ref, w_ref, o_ref, *, passes):
    if passes == 1:
        o_ref[...] = jnp.dot(x_ref[...].astype(BF16), w_ref[...].astype(BF16),
                             preferred_element_type=F32)
    else:
        o_ref[...] = _dot_hi(x_ref[...], w_ref[...])


def _matmul(x, w, *, passes=1, tm=1024):
    t, k = x.shape
    n = w.shape[1]
    tm = min(tm, t)
    tn = n if n <= 1024 else (1024 if n % 1024 == 0 else 512)
    assert t % tm == 0 and n % tn == 0
    return pl.pallas_call(
        functools.partial(_mm_kernel, passes=passes),
        grid=(t // tm, n // tn),
        in_specs=[pl.BlockSpec((tm, k), lambda i, j: (i, 0)),
                  pl.BlockSpec((k, tn), lambda i, j: (0, j))],
        out_specs=pl.BlockSpec((tm, tn), lambda i, j: (i, j)),
        out_shape=jax.ShapeDtypeStruct((t, n), F32),
        compiler_params=_cparams(("parallel", "parallel")),
        name="proj_matmul",
    )(x, w)


def _conv_kernel(prev_ref, main_ref, next_ref, w_ref, b_ref, o_ref, buf_ref, *,
                 tl, n_norm_blocks, n_scale_blocks, head_dim, scale):
    i = pl.program_id(1)
    j = pl.program_id(2)
    last = pl.num_programs(1) - 1
    keep_prev = jnp.where(i == 0, 0.0, 1.0)
    keep_next = jnp.where(i == last, 0.0, 1.0)
    buf_ref[0:8, :] = prev_ref[0, 0] * keep_prev
    buf_ref[8:8 + tl, :] = main_ref[0]
    buf_ref[8 + tl:16 + tl, :] = next_ref[0, 0] * keep_next
    w = w_ref[...]
    acc = None
    for d in range(CONV_K):
        term = buf_ref[6 + d:6 + d + tl, :] * w[d:d + 1, :]
        acc = term if acc is None else acc + term
    y = _silu(acc + b_ref[...])
    if n_norm_blocks > 0:
        do_norm = j < n_norm_blocks
        fac = jnp.where(j < n_scale_blocks, scale, 1.0)
        pieces = []
        for h in range(y.shape[1] // head_dim):
            yh = y[:, h * head_dim:(h + 1) * head_dim]
            ss = jnp.sum(yh * yh, axis=1, keepdims=True)
            r = jnp.where(do_norm, lax.rsqrt(ss + NORM_EPS) * fac, 1.0)
            pieces.append(yh * r)
        y = jnp.concatenate(pieces, axis=1)
    o_ref[0] = y


def _conv_silu(proj, lane0, width, w, bias, *, n_norm_blocks=0, n_scale_blocks=0,
               head_dim=128, scale=1.0, tc=512):
    b, l, n = proj.shape
    tl = min(512, l)
    assert l % tl == 0 and width % tc == 0 and lane0 % tc == 0 and tl % 8 == 0
    off = lane0 // tc
    halo = proj.reshape(b, l // 8, 8, n)
    nl8 = l // 8
    step8 = tl // 8
    return pl.pallas_call(
        functools.partial(_conv_kernel, tl=tl, n_norm_blocks=n_norm_blocks,
                          n_scale_blocks=n_scale_blocks, head_dim=head_dim, scale=scale),
        grid=(b, l // tl, width // tc),
        in_specs=[
            pl.BlockSpec((1, 1, 8, tc), lambda bi, i, j: (bi, jnp.maximum(i * step8 - 1, 0), 0, off + j)),
            pl.BlockSpec((1, tl, tc), lambda bi, i, j: (bi, i, off + j)),
            pl.BlockSpec((1, 1, 8, tc), lambda bi, i, j: (bi, jnp.minimum((i + 1) * step8, nl8 - 1), 0, off + j)),
            pl.BlockSpec((CONV_K, tc), lambda bi, i, j: (0, j)),
            pl.BlockSpec((1, tc), lambda bi, i, j: (0, j)),
        ],
        out_specs=pl.BlockSpec((1, tl, tc), lambda bi, i, j: (bi, i, j)),
        out_shape=jax.ShapeDtypeStruct((b, l, width), F32),
        scratch_shapes=[pltpu.VMEM((tl + 16, tc), F32)],
        compiler_params=_cparams(("parallel", "parallel", "parallel")),
        name="conv_silu",
    )(halo, proj, halo, w, bias)


GATE_ROWS = 512


def _gate_prep_kernel(raw_ref, bias_ref, alog_ref, tri_l_ref, tri_u_ref, ones_ref,
                      pf_ref, pb_ref, of_ref, ob_ref, *, aux_is_sigmoid):
    raw = raw_ref[...]
    sp = _softplus(raw + bias_ref[...])
    g = -jnp.exp(alog_ref[...]) * sp
    aux = jax.nn.sigmoid(raw) if aux_is_sigmoid else sp
    gc_l = _dot_sel_l(tri_l_ref[...], g)
    gc_u = _dot_sel_l(tri_u_ref[...], g)
    gtot = _dot_sel_l(ones_ref[...], g)
    pf = pf_ref[...]
    pb = pb_ref[...]
    of_ref[...] = (_dot_sel_r(gc_l, pf[0]) + _dot_sel_r(aux, pf[1]) + _dot_sel_r(gtot, pf[2]))
    ob_ref[...] = (_dot_sel_r(gc_u, pb[0]) + _dot_sel_r(aux, pb[1]) + _dot_sel_r(gtot, pb[2]))


def _chunk_masks(rows):
    r = np.arange(rows)
    same = (r[:, None] // CHUNK) == (r[None, :] // CHUNK)
    tri_l = same & (r[None, :] <= r[:, None])
    tri_u = same & (r[None, :] >= r[:, None])
    return (jnp.asarray(tri_l, BF16), jnp.asarray(tri_u, BF16), jnp.asarray(same, BF16))


def _placement(src_gc, src_aux, n_slab, per_slab):
    p = np.zeros((3, 64, n_slab * 128), np.float32)
    for h in range(n_slab * per_slab):
        base = (h // per_slab) * 128 + h % per_slab
        p[0, src_gc[h], base] = 1.0
        p[1, src_aux[h], base + 32] = 1.0
        p[2, src_gc[h], base + 64] = 1.0
    return jnp.asarray(p, BF16)


def _gate_prep(raw, bias64, alog64, pf, pb, *, aux_is_sigmoid):
    t = raw.shape[0]
    rows = min(GATE_ROWS, t)
    w = pf.shape[2]
    tri_l, tri_u, ones = _chunk_masks(rows)
    full = lambda shape: pl.BlockSpec(shape, lambda i: (0,) * len(shape))
    return pl.pallas_call(
        functools.partial(_gate_prep_kernel, aux_is_sigmoid=aux_is_sigmoid),
        grid=(t // rows,),
        in_specs=[pl.BlockSpec((rows, 64), lambda i: (i, 0)),
                  full((1, 64)), full((1, 64)),
                  full((rows, rows)), full((rows, rows)), full((rows, rows)),
                  full((3, 64, w)), full((3, 64, w))],
        out_specs=[pl.BlockSpec((rows, w), lambda i: (i, 0)),
                   pl.BlockSpec((rows, w), lambda i: (i, 0))],
        out_shape=[jax.ShapeDtypeStruct((t, w), F32), jax.ShapeDtypeStruct((t, w), F32)],
        compiler_params=_cparams(("parallel",)),
        name="gate_prep",
    )(raw, bias64, alog64, tri_l, tri_u, ones, pf, pb)


def _row_form(col, b, l):
    w = col.shape[1]
    return jnp.swapaxes(col.reshape(b, l // CHUNK, CHUNK, w), 2, 3)


def _tri_masks(rev):
    ri = _iota2((CHUNK, CHUNK), 0)
    ci = _iota2((CHUNK, CHUNK), 1)
    if rev:
        return ri, ci, ci >= ri, ci > ri
    return ri, ci, ci <= ri, ci < ri


GDN_SLAB = 4
GDN_SLAB_ROWS = GDN_SLAB * CHUNK


def _level_masks(n, rev):
    ri = _iota2((n, n), 0)
    ci = _iota2((n, n), 1)
    masks = []
    s = 1
    while s < CHUNK:
        shift = int(math.log2(2 * s))
        same = (ri >> shift) == (ci >> shift)
        r_hi = (ri & (2 * s - 1)) >= s
        c_hi = (ci & (2 * s - 1)) >= s
        masks.append(same & (~r_hi) & c_hi if rev else same & r_hi & (~c_hi))
        s *= 2
    return ri, ci, masks


def _unit_tri_inverse(a, eye, masks):
    t = eye - jnp.where(masks[0], a, 0.0)
    for m in masks[1:]:
        tb = t.astype(BF16)
        x = jnp.dot(tb, jnp.where(m, a, 0.0).astype(BF16), preferred_element_type=F32)
        t = t - jnp.dot(x.astype(BF16), tb, preferred_element_type=F32)
    return t


def _gdn_scan_kernel(q_ref, k_ref, v_ref, gcol_ref, grow_ref, o_ref, s_ref, *, rev):
    @pl.when(pl.program_id(1) == 0)
    def _():
        s_ref[...] = jnp.zeros_like(s_ref)

    c = CHUNK
    d = GDN_HEAD_DIM
    n = GDN_SLAB_ROWS
    ri, ci, masks = _level_masks(n, rev)
    same_head = (ri >> 6) == (ci >> 6)
    if rev:
        incl = same_head & (ci >= ri)
        strict = same_head & (ci > ri)
    else:
        incl = same_head & (ci <= ri)
        strict = same_head & (ci < ri)
    eye = jnp.where(ri == ci, 1.0, 0.0)
    first = _iota2((c, 2 * d), 1) < d
    pair_rows = _iota2((2 * c, 2 * d), 0) < c
    pair_lanes = _iota2((2 * c, 2 * d), 1) < d
    vn_mask = pair_rows == pair_lanes
    state_mask = (_iota2((2 * d, 2 * d), 0) < d) == (_iota2((2 * d, 2 * d), 1) < d)

    for j in range(GDN_HEADS // GDN_SLAB):
        gcol = gcol_ref[0][:, 128 * j:128 * (j + 1)]
        grow = grow_ref[0, 0][j:j + 1, :]
        heads = range(GDN_SLAB)
        qk_head = [(GDN_SLAB * j + i) // 2 for i in heads]
        q_st = jnp.concatenate([q_ref[0][:, d * h:d * (h + 1)] for h in qk_head], axis=0)
        k_st = jnp.concatenate([k_ref[0][:, d * h:d * (h + 1)] for h in qk_head], axis=0)
        v_st = jnp.concatenate([v_ref[0][:, d * (GDN_SLAB * j + i):d * (GDN_SLAB * j + i + 1)]
                                for i in heads], axis=0)
        gc_c = jnp.concatenate([gcol[:, i:i + 1] for i in heads], axis=0)
        beta_c = jnp.concatenate([gcol[:, 32 + i:33 + i] for i in heads], axis=0)
        decay = jnp.exp(jnp.where(incl, gc_c - grow, NEG_BIG))
        kb = k_st.astype(BF16)
        kk = lax.dot_general(kb, kb, _NT, preferred_element_type=F32)
        p = lax.dot_general(q_st.astype(BF16), kb, _NT, preferred_element_type=F32) * decay
        a = jnp.where(strict, kk * decay, 0.0) * beta_c
        t = _unit_tri_inverse(a, eye, masks)
        e_gc = jnp.exp(gc_c)
        rhs = jnp.concatenate([v_st * beta_c, k_st * (beta_c * e_gc)], axis=1)
        uw = _dot(t, rhs)
        for pr in range(GDN_SLAB // 2):
            ha, hb = 2 * pr, 2 * pr + 1
            ra = slice(c * ha, c * (ha + 1))
            rb = slice(c * hb, c * (hb + 1))
            u_pair = jnp.concatenate([uw[ra, :d], uw[rb, :d]], axis=1)
            w_pair = jnp.concatenate([uw[ra, d:], uw[rb, d:]], axis=1)
            q_pair = jnp.concatenate([q_st[ra], q_st[rb]], axis=1)
            k_pair = jnp.concatenate([k_st[ra], k_st[rb]], axis=1)
            gc_pair = jnp.where(first, gcol[:, ha:ha + 1], gcol[:, hb:hb + 1])
            gt_pair = jnp.where(first, gcol[:, 64 + ha:65 + ha], gcol[:, 64 + hb:65 + hb])
            idx = (GDN_SLAB // 2) * j + pr
            s = s_ref[idx]
            wq = _dot(jnp.concatenate([w_pair, q_pair * jnp.exp(gc_pair)], axis=0), s)
            v_new = u_pair - wq[:c]
            p_pair = p[ra, 2 * c * pr:2 * c * (pr + 1)] + p[rb, 2 * c * pr:2 * c * (pr + 1)]
            vn_bd = jnp.where(vn_mask, jnp.concatenate([v_new, v_new], axis=0), 0.0)
            o_ref[0, :, 2 * d * idx:2 * d * (idx + 1)] = wq[c:] + _dot(p_pair, vn_bd)
            k_dec = k_pair * jnp.exp(gt_pair - gc_pair)
            s_ref[idx] = (s * jnp.exp(gt_pair[0:1, :])
                          + jnp.where(state_mask, _dot(k_dec, v_new, _TN), 0.0))


def _gdn_row_form(col, b, l):
    nslab = GDN_HEADS // GDN_SLAB
    g = col.reshape(b, l // CHUNK, CHUNK, nslab, 128)[..., :GDN_SLAB]
    g = jnp.transpose(g, (0, 1, 3, 4, 2)).reshape(b, l // CHUNK, nslab, GDN_SLAB_ROWS)
    return jnp.pad(g, ((0, 0), (0, 0), (0, 8 - nslab), (0, 0)))


def _gdn_scan(qkv, col, row, *, rev):
    b, l, _ = qkv.shape
    nc = l // CHUNK
    kw = GDN_HEADS // 2 * GDN_HEAD_DIM
    vw = GDN_HEADS * GDN_HEAD_DIM
    cidx = (lambda n: nc - 1 - n) if rev else (lambda n: n)
    return pl.pallas_call(
        functools.partial(_gdn_scan_kernel, rev=rev),
        grid=(b, nc),
        in_specs=[
            pl.BlockSpec((1, CHUNK, kw), lambda bi, n: (bi, cidx(n), 0)),
            pl.BlockSpec((1, CHUNK, kw), lambda bi, n: (bi, cidx(n), 1)),
            pl.BlockSpec((1, CHUNK, vw), lambda bi, n: (bi, cidx(n), 1)),
            pl.BlockSpec((1, CHUNK, col.shape[2]), lambda bi, n: (bi, cidx(n), 0)),
            pl.BlockSpec((1, 1, 8, GDN_SLAB_ROWS), lambda bi, n: (bi, cidx(n), 0, 0)),
        ],
        out_specs=pl.BlockSpec((1, CHUNK, vw), lambda bi, n: (bi, cidx(n), 0)),
        out_shape=jax.ShapeDtypeStruct((b, l, vw), F32),
        scratch_shapes=[pltpu.VMEM((GDN_HEADS // 2, 2 * GDN_HEAD_DIM, 2 * GDN_HEAD_DIM), F32)],
        compiler_params=_cparams(("parallel", "arbitrary")),
        name="gdn_scan_bwd" if rev else "gdn_scan_fwd",
    )(qkv, qkv, qkv, col, row)


def _ssd_scan_kernel(c_ref, b_ref, x_ref, gcol_ref, grow_ref, e8_ref, y_ref, s_ref, *, rev):
    @pl.when(pl.program_id(1) == 0)
    def _():
        s_ref[...] = jnp.zeros_like(s_ref)

    _, _, incl, _ = _tri_masks(rev)
    e8 = e8_ref[...]
    r_heads = M2_HEADS_PER_GROUP
    gw = r_heads * M2_HEAD_DIM
    lane_head = _iota2((CHUNK, gw), 1) // M2_HEAD_DIM
    for g in range(M2_GROUPS):
        cc = c_ref[0][:, M2_STATE * g:M2_STATE * (g + 1)]
        bb = b_ref[0][:, M2_STATE * g:M2_STATE * (g + 1)]
        xs = x_ref[0][:, gw * g:gw * (g + 1)]
        gcol = gcol_ref[0][:, 128 * g:128 * (g + 1)]
        grow = grow_ref[0, 0][128 * g:128 * (g + 1), :]
        gc_x = _dot_sel_r(gcol[:, 0:r_heads], e8)
        dt_x = _dot_sel_r(gcol[:, 32:32 + r_heads], e8)
        gt_x = _dot_sel_r(gcol[:, 64:64 + r_heads], e8)
        u = xs * dt_x
        scores = _dot(cc, bb, _NT)
        s = s_ref[g]
        y = jnp.exp(gc_x) * _dot(cc, s)
        for r in range(r_heads):
            decay = jnp.exp(jnp.where(incl, gcol[:, r:r + 1] - grow[r:r + 1, :], NEG_BIG))
            y = y + _dot(scores * decay, jnp.where(lane_head == r, u, 0.0))
        s_ref[g] = s * jnp.exp(gt_x[0:1, :]) + _dot(bb, u * jnp.exp(gt_x - gc_x), _TN)
        y_ref[0, :, gw * g:gw * (g + 1)] = y


def _ssd_scan(xbc, col, row, e8, *, rev):
    b, l, _ = xbc.shape
    nc = l // CHUNK
    gw = M2_HEADS_PER_GROUP * M2_HEAD_DIM
    xw = M2_GROUPS * gw
    sw = M2_GROUPS * M2_STATE
    cidx = (lambda n: nc - 1 - n) if rev else (lambda n: n)
    return pl.pallas_call(
        functools.partial(_ssd_scan_kernel, rev=rev),
        grid=(b, nc),
        in_specs=[
            pl.BlockSpec((1, CHUNK, sw), lambda bi, n: (bi, cidx(n), xw // sw + 1)),
            pl.BlockSpec((1, CHUNK, sw), lambda bi, n: (bi, cidx(n), xw // sw)),
            pl.BlockSpec((1, CHUNK, xw), lambda bi, n: (bi, cidx(n), 0)),
            pl.BlockSpec((1, CHUNK, M2_GROUPS * 128), lambda bi, n: (bi, cidx(n), 0)),
            pl.BlockSpec((1, 1, M2_GROUPS * 128, CHUNK), lambda bi, n: (bi, cidx(n), 0, 0)),
            pl.BlockSpec((M2_HEADS_PER_GROUP, gw), lambda bi, n: (0, 0)),
        ],
        out_specs=pl.BlockSpec((1, CHUNK, xw), lambda bi, n: (bi, cidx(n), 0)),
        out_shape=jax.ShapeDtypeStruct((b, l, xw), F32),
        scratch_shapes=[pltpu.VMEM((M2_GROUPS, M2_STATE, gw), F32)],
        compiler_params=_cparams(("parallel", "arbitrary")),
        name="ssd_scan_bwd" if rev else "ssd_scan_fwd",
    )(xbc, xbc, xbc, col, row, e8)


def _gla_chunk(q, k, v, g, st, rev):
    c = CHUNK
    ri, ci, incl, _ = _tri_masks(rev)
    gsum = _dot_sel_l(jnp.where(incl, 1.0, 0.0), g)
    gtot = gsum[0:1] if rev else gsum[c - 1:c]
    o = _dot(q * jnp.exp(gsum), st, _NT)
    st_new = st * jnp.exp(gtot) + _dot(v, k * jnp.exp(gtot - gsum), _TN)

    row = _iota2((c, 1), 0)
    attn = jnp.zeros((c, c), F32)
    s = c // 2
    while s >= 8:
        pieces = []
        for p in range(c // (2 * s)):
            r = p * 2 * s + (s if rev else s - 1)
            pieces.append(jnp.broadcast_to(gsum[r:r + 1], (2 * s, gsum.shape[1])))
        ref = pieces[0] if len(pieces) == 1 else jnp.concatenate(pieces, axis=0)
        e = jnp.exp(-jnp.abs(gsum - ref))
        upper_half = (row & (2 * s - 1)) >= s
        q_rows = (~upper_half) if rev else upper_half
        qs = jnp.where(q_rows, q * e, 0.0)
        ks = jnp.where(q_rows, 0.0, k * e)
        shift = int(math.log2(2 * s))
        same_pair = (ri >> shift) == (ci >> shift)
        attn = attn + jnp.where(same_pair, _dot(qs, ks, _NT), 0.0)
        s //= 2

    lane = _iota2((8, c), 1)
    sub = _iota2((8, 1), 0)
    blocks = []
    for blk in range(c // 8):
        qb = q[8 * blk:8 * blk + 8]
        kb = k[8 * blk:8 * blk + 8]
        gb = gsum[8 * blk:8 * blk + 8]
        acc = jnp.zeros((8, c), F32)
        for j in range(8):
            e = jnp.exp(jnp.minimum(gb - gb[j:j + 1], 0.0))
            red = jnp.sum(qb * kb[j:j + 1] * e, axis=1, keepdims=True)
            ok = (sub <= j) if rev else (sub >= j)
            acc = jnp.where((lane == 8 * blk + j) & ok, red, acc)
        blocks.append(acc)
    attn = attn + jnp.concatenate(blocks, axis=0)
    o = o + _dot(attn, v)
    return o, st_new


HG_HB = 8
HG_HEADS = 8
HG_DIM = 128


def _hg_scan_kernel(q_ref, f_ref, v_ref, lb_ref, o_ref, st_ref, *, hb, rev):
    @pl.when(pl.program_id(2) == 0)
    def _():
        st_ref[...] = jnp.zeros_like(st_ref)

    lbp = lb_ref[...]
    d = HG_DIM
    outs = []
    for i in range(hb):
        sl = slice(i * d, (i + 1) * d)
        q = _silu(q_ref[0][:, sl])
        f = f_ref[0][:, sl]
        k = lbp[0:1, sl] * jax.nn.sigmoid(-f)
        a = lbp[1:2, sl]
        bterm = lbp[2:3, sl] + _log_sigmoid(f)
        g = jnp.maximum(a, bterm) + jnp.log1p(jnp.exp(-jnp.abs(a - bterm)))
        o, st_new = _gla_chunk(q, k, v_ref[0][:, sl], g, st_ref[i], rev)
        st_ref[i] = st_new
        outs.append(o)
    o_ref[0] = jnp.concatenate(outs, axis=1)


def _hg_scan(proj, lbp, *, rev):
    b, l, _ = proj.shape
    nc = l // CHUNK
    hb = HG_HB
    nsl = HG_HEADS // hb
    f_off = (2 if rev else 1) * nsl
    cidx = (lambda n: nc - 1 - n) if rev else (lambda n: n)
    blk = (1, CHUNK, hb * HG_DIM)
    return pl.pallas_call(
        functools.partial(_hg_scan_kernel, hb=hb, rev=rev),
        grid=(b, nsl, nc),
        in_specs=[
            pl.BlockSpec(blk, lambda bi, j, n: (bi, cidx(n), j)),
            pl.BlockSpec(blk, lambda bi, j, n: (bi, cidx(n), f_off + j)),
            pl.BlockSpec(blk, lambda bi, j, n: (bi, cidx(n), 3 * nsl + j)),
            pl.BlockSpec((8, hb * HG_DIM), lambda bi, j, n: (0, j)),
        ],
        out_specs=pl.BlockSpec(blk, lambda bi, j, n: (bi, cidx(n), j)),
        out_shape=jax.ShapeDtypeStruct((b, l, HG_HEADS * HG_DIM), F32),
        scratch_shapes=[pltpu.VMEM((hb, HG_DIM, HG_DIM), F32)],
        compiler_params=_cparams(("parallel", "parallel", "arbitrary")),
        name="hgrn2_scan_bwd" if rev else "hgrn2_scan_fwd",
    )(proj, proj, proj, lbp)


GLA_HB = 4
GLA_HEADS = 4
GLA_K = 128
GLA_V = 256


def _gla_scan_kernel(q_ref, k_ref, v_ref, r_ref, wgk_ref, bgk_ref, o_ref, st_ref, *, hb, rev):
    @pl.when(pl.program_id(2) == 0)
    def _():
        st_ref[...] = jnp.zeros_like(st_ref)

    pre = _dot_hi(r_ref[0], wgk_ref[...]) + bgk_ref[...]
    gk = _log_sigmoid(pre) * (1.0 / GLA_GATE_TEMP)
    outs = []
    for i in range(hb):
        sl = slice(i * GLA_K, (i + 1) * GLA_K)
        q = q_ref[0][:, sl] * (GLA_K ** -0.5)
        o, st_new = _gla_chunk(q, k_ref[0][:, sl], v_ref[0][:, i * GLA_V:(i + 1) * GLA_V],
                               gk[:, sl], st_ref[i], rev)
        st_ref[i] = st_new
        outs.append(o)
    o_ref[0] = jnp.concatenate(outs, axis=1)


def _gla_scan(proj, r, wgk, bgk, *, rev):
    b, l, _ = proj.shape
    nc = l // CHUNK
    hb = GLA_HB
    nsl = GLA_HEADS // hb
    cidx = (lambda n: nc - 1 - n) if rev else (lambda n: n)
    return pl.pallas_call(
        functools.partial(_gla_scan_kernel, hb=hb, rev=rev),
        grid=(b, nsl, nc),
        in_specs=[
            pl.BlockSpec((1, CHUNK, hb * GLA_K), lambda bi, j, n: (bi, cidx(n), j)),
            pl.BlockSpec((1, CHUNK, hb * GLA_K), lambda bi, j, n: (bi, cidx(n), nsl + j)),
            pl.BlockSpec((1, CHUNK, hb * GLA_V), lambda bi, j, n: (bi, cidx(n), nsl + j)),
            pl.BlockSpec((1, CHUNK, r.shape[2]), lambda bi, j, n: (bi, cidx(n), 0)),
            pl.BlockSpec((wgk.shape[0], hb * GLA_K), lambda bi, j, n: (0, j)),
            pl.BlockSpec((1, hb * GLA_K), lambda bi, j, n: (0, j)),
        ],
        out_specs=pl.BlockSpec((1, CHUNK, hb * GLA_V), lambda bi, j, n: (bi, cidx(n), j)),
        out_shape=jax.ShapeDtypeStruct((b, l, GLA_HEADS * GLA_V), F32),
        scratch_shapes=[pltpu.VMEM((hb, GLA_V, GLA_K), F32)],
        compiler_params=_cparams(("parallel", "parallel", "arbitrary")),
        name="gla_scan_bwd" if rev else "gla_scan_fwd",
    )(proj, proj, proj, r, wgk, bgk)


def _layernorm_rows(t, g, b):
    mu = jnp.mean(t, axis=1, keepdims=True)
    tc = t - mu
    var = jnp.mean(tc * tc, axis=1, keepdims=True)
    return tc * lax.rsqrt(var + LN_EPS) * g + b


def _out_ln_kernel(*refs, group, gate_first, has_skip):
    if has_skip:
        of_ref, ob_ref, z_ref, xs_ref, d_ref, nw_ref, w_ref, x_ref, g_ref, b_ref, o_ref = refs
        y = of_ref[...] + ob_ref[...] + d_ref[...] * xs_ref[...]
    else:
        of_ref, ob_ref, z_ref, nw_ref, w_ref, x_ref, g_ref, b_ref, o_ref = refs
        y = of_ref[...] + ob_ref[...]
    gate = _silu(z_ref[...])
    if gate_first:
        y = y * gate
    pieces = []
    for h in range(y.shape[1] // group):
        yh = y[:, h * group:(h + 1) * group]
        ms = jnp.mean(yh * yh, axis=1, keepdims=True)
        pieces.append(yh * lax.rsqrt(ms + NORM_EPS))
    y = jnp.concatenate(pieces, axis=1) * nw_ref[...]
    if not gate_first:
        y = y * gate
    h_out = _dot(y, w_ref[...])
    o_ref[...] = _layernorm_rows(DEEPNORM_ALPHA * x_ref[...] + h_out, g_ref[...], b_ref[...])


def _out_ln(o_f, o_b, z_arr, z_block, nw_full, w_out, x_res, ln_g, ln_b, *, group, gate_first,
            skip=None, tm=256):
    t, din = o_f.shape
    d = w_out.shape[1]
    tm = min(tm, t)
    row = lambda width, blk=0: pl.BlockSpec((tm, width), lambda i: (i, blk))
    full = lambda shape: pl.BlockSpec(shape, lambda i: (0,) * len(shape))
    ins = [o_f, o_b, z_arr]
    specs = [row(din), row(din), row(din, z_block)]
    if skip is not None:
        xs_arr, d_full = skip
        ins += [xs_arr, d_full]
        specs += [row(din), full((1, din))]
    ins += [nw_full, w_out, x_res, ln_g, ln_b]
    specs += [full((1, din)), full((din, d)), row(d), full((1, d)), full((1, d))]
    return pl.pallas_call(
        functools.partial(_out_ln_kernel, group=group, gate_first=gate_first, has_skip=skip is not None),
        grid=(t // tm,),
        in_specs=specs,
        out_specs=row(d),
        out_shape=jax.ShapeDtypeStruct((t, d), F32),
        compiler_params=_cparams(("parallel",)),
        name="gated_norm_out_ln",
    )(*ins)


def _router_kernel(x_ref, wt_ref, bias_ref, tri_ref, row_ref, col_ref, cnt_ref):
    x = x_ref[...]
    tm = x.shape[0]
    logits = _dot_hi(wt_ref[...], x, _NT) + bias_ref[...]
    sub8 = _iota2((8, tm), 0).astype(F32)
    gl = jnp.where(sub8 < MOE_GROUPS, logits[0:8], NEG_BIG)
    gmax = jnp.max(gl, axis=0, keepdims=True)
    gidx = jnp.min(jnp.where(gl == gmax, sub8, 8.0), axis=0, keepdims=True)
    gprob = 1.0 / jnp.sum(jnp.exp(gl - gmax), axis=0, keepdims=True)
    el = logits[8:16]
    for g in range(1, MOE_GROUPS):
        el = jnp.where(gidx == g, logits[8 + 8 * g:16 + 8 * g], el)
    m1 = jnp.max(el, axis=0, keepdims=True)
    i1 = jnp.min(jnp.where(el == m1, sub8, 8.0), axis=0, keepdims=True)
    el2 = jnp.where(sub8 == i1, NEG_BIG, el)
    m2 = jnp.max(el2, axis=0, keepdims=True)
    i2 = jnp.min(jnp.where(el2 == m2, sub8, 8.0), axis=0, keepdims=True)
    e2 = jnp.exp(m2 - m1)
    w1 = gprob / (1.0 + e2)
    w2 = gprob * e2 / (1.0 + e2)
    within = jnp.where(sub8 == i1, w1, 0.0) + jnp.where(sub8 == i2, w2, 0.0)
    onehot = jnp.where(sub8 == gidx, 1.0, 0.0)
    before = _dot(onehot, tri_ref[...])
    pos = jnp.sum(onehot * before, axis=0, keepdims=True)
    rec = jnp.where(sub8 == 0, gidx, jnp.where(sub8 == 1, pos, 0.0))
    rows = jnp.concatenate([within, rec], axis=0)
    row_ref[...] = rows
    padded = jnp.concatenate([rows, jnp.zeros((128 - ROUTE_ROWS, tm), F32)], axis=0)
    col_ref[...] = padded.T
    cnt_ref[0] = jnp.broadcast_to(jnp.sum(onehot, axis=1, keepdims=True), (8, 128))


def _router(x, wt, bias_col):
    t, d = x.shape
    tm = min(MOE_TILE, t)
    nt = t // tm
    r = np.arange(tm)
    tri = jnp.asarray(r[:, None] < r[None, :], BF16)
    full = lambda shape: pl.BlockSpec(shape, lambda i: (0,) * len(shape))
    return pl.pallas_call(
        _router_kernel,
        grid=(nt,),
        in_specs=[pl.BlockSpec((tm, d), lambda i: (i, 0)), full((128, d)), full((128, 1)), full((tm, tm))],
        out_specs=[pl.BlockSpec((ROUTE_ROWS, tm), lambda i: (0, i)),
                   pl.BlockSpec((tm, 128), lambda i: (i, 0)),
                   pl.BlockSpec((1, 8, 128), lambda i: (i, 0, 0))],
        out_shape=[jax.ShapeDtypeStruct((ROUTE_ROWS, t), F32),
                   jax.ShapeDtypeStruct((t, 128), F32),
                   jax.ShapeDtypeStruct((nt, 8, 128), F32)],
        compiler_params=_cparams(("parallel",)),
        name="moe_router",
    )(x, wt, bias_col, tri)


def _moe_kernel(cnt_ref, x_ref, row_ref, col_ref, wg_ref, wu_ref, wd_ref, o_ref, *, cap):
    g = pl.program_id(0)
    ti = pl.program_id(1)
    n = cnt_ref[ti * MOE_GROUPS + g]
    nb = (n + cap - 1) // cap
    tm = x_ref.shape[0]
    gf = g.astype(F32)
    rows = row_ref[...]
    cols = col_ref[...]
    in_g_row = rows[8:9, :] == gf
    pos_row = rows[9:10, :]
    in_g_col = cols[:, 8:9] == gf
    pos_col = cols[:, 9:10]
    xb = x_ref[...].astype(BF16)
    o_ref[0] = jnp.zeros((tm, o_ref.shape[2]), F32)

    def body(bi, carry):
        base = (bi * cap).astype(F32)
        slot_r = _iota2((cap, tm), 0).astype(F32) + base
        sel = jnp.where(in_g_row & (slot_r == pos_row), 1.0, 0.0).astype(BF16)
        xg = jnp.dot(sel, xb, preferred_element_type=F32).astype(BF16)
        cw = _dot_sel_l(sel, cols)
        y = jnp.zeros((cap, o_ref.shape[2]), F32)
        for e in range(MOE_EPG):
            hg = jnp.dot(xg, wg_ref[0, e], preferred_element_type=F32)
            hu = jnp.dot(xg, wu_ref[0, e], preferred_element_type=F32)
            h = _silu(hg) * hu * cw[:, e:e + 1]
            y = y + _dot(h, wd_ref[0, e])
        slot_c = _iota2((tm, cap), 1).astype(F32) + base
        sel_t = jnp.where(in_g_col & (slot_c == pos_col), 1.0, 0.0).astype(BF16)
        o_ref[0] += jnp.dot(sel_t, y.astype(BF16), preferred_element_type=F32)
        return carry

    lax.fori_loop(0, nb, body, 0)


def _moe(x, route_row, route_col, counts, wg, wu, wd):
    t, d = x.shape
    tm = min(MOE_TILE, t)
    cap = min(MOE_CAP, tm)
    nt = t // tm
    hid = wg.shape[3]
    grid_spec = pltpu.PrefetchScalarGridSpec(
        num_scalar_prefetch=1,
        grid=(MOE_GROUPS, nt),
        in_specs=[
            pl.BlockSpec((tm, d), lambda g, i, c: (i, 0)),
            pl.BlockSpec((ROUTE_ROWS, tm), lambda g, i, c: (0, i)),
            pl.BlockSpec((tm, 128), lambda g, i, c: (i, 0)),
            pl.BlockSpec((1, MOE_EPG, d, hid), lambda g, i, c: (g, 0, 0, 0)),
            pl.BlockSpec((1, MOE_EPG, d, hid), lambda g, i, c: (g, 0, 0, 0)),
            pl.BlockSpec((1, MOE_EPG, hid, d), lambda g, i, c: (g, 0, 0, 0)),
        ],
        out_specs=pl.BlockSpec((1, tm, d), lambda g, i, c: (g, i, 0)),
    )
    return pl.pallas_call(
        functools.partial(_moe_kernel, cap=cap),
        grid_spec=grid_spec,
        out_shape=jax.ShapeDtypeStruct((MOE_GROUPS, t, d), F32),
        compiler_params=_cparams(("arbitrary", "arbitrary")),
        name="moe_experts",
    )(counts, x, route_row, route_col, wg, wu, wd)


def _ln_pe_kernel(x_ref, m_ref, p_ref, g_ref, b_ref, wg_ref, wp_ref, o_ref):
    h = m_ref[0]
    for g in range(1, MOE_GROUPS):
        h = h + m_ref[g]
    y = _layernorm_rows(DEEPNORM_ALPHA * x_ref[...] + h, g_ref[...], b_ref[...])
    gate = jax.nn.sigmoid(_dot(y, wg_ref[...]))
    o_ref[...] = y + gate * _dot(p_ref[...], wp_ref[...])


def _ln_pe(x, moe_parts, p, ln_g, ln_b, w_gate, w_proj, *, tm=512):
    t, d = x.shape
    pd = p.shape[1]
    tm = min(tm, t)
    full = lambda shape: pl.BlockSpec(shape, lambda i: (0,) * len(shape))
    return pl.pallas_call(
        _ln_pe_kernel,
        grid=(t // tm,),
        in_specs=[pl.BlockSpec((tm, d), lambda i: (i, 0)),
                  pl.BlockSpec((MOE_GROUPS, tm, d), lambda i: (0, i, 0)),
                  pl.BlockSpec((tm, pd), lambda i: (i, 0)),
                  full((1, d)), full((1, d)), full((d, d)), full((pd, d))],
        out_specs=pl.BlockSpec((tm, d), lambda i: (i, 0)),
        out_shape=jax.ShapeDtypeStruct((t, d), F32),
        compiler_params=_cparams(("parallel",)),
        name="ln_input_embedding",
    )(x, moe_parts, p, ln_g, ln_b, w_gate, w_proj)


def _pad64(v):
    v = v.reshape(1, -1).astype(F32)
    return jnp.pad(v, ((0, 0), (0, 64 - v.shape[1])))


def _gdn_mixer(x2, b, l, w_in, conv_w, a_log, dt_bias, norm_w, w_out, x_res, ln_g, ln_b):
    t = b * l
    conv_ch = 4096
    main = _matmul(x2, w_in[:, :6144].astype(BF16))
    ab = _matmul(x2, w_in[:, 6144:], passes=3)
    qkv = _conv_silu(main.reshape(b, l, 6144), 0, conv_ch, conv_w.astype(F32),
                     jnp.zeros((1, conv_ch), F32), n_norm_blocks=4, n_scale_blocks=2,
                     head_dim=GDN_HEAD_DIM, scale=GDN_HEAD_DIM ** -0.5)
    per_slab = GDN_SLAB
    nslab = GDN_HEADS // per_slab
    heads = np.arange(GDN_HEADS)
    pf = _placement(heads, 32 + heads, nslab, per_slab)
    pb = _placement(16 + heads, 48 + heads, nslab, per_slab)
    col_f, col_b = _gate_prep(ab, _pad64(dt_bias), _pad64(a_log), pf, pb, aux_is_sigmoid=True)
    o_f = _gdn_scan(qkv, col_f.reshape(b, l, -1), _gdn_row_form(col_f, b, l), rev=False)
    o_b = _gdn_scan(qkv, col_b.reshape(b, l, -1), _gdn_row_form(col_b, b, l), rev=True)
    nw = jnp.tile(norm_w.astype(F32), GDN_HEADS).reshape(1, -1)
    return _out_ln(o_f.reshape(t, -1), o_b.reshape(t, -1), main, 2, nw, w_out.astype(BF16),
                   x_res, ln_g, ln_b, group=GDN_HEAD_DIM, gate_first=False)


def _mamba2_mixer(x2, b, l, w_in, conv_w, conv_b, a_log, dt_bias, d_skip, norm_w, w_out,
                  x_res, ln_g, ln_b):
    t = b * l
    d_inner = 2048
    conv_ch = 3072
    main = _matmul(x2, w_in[:, :d_inner + conv_ch].astype(BF16))
    dt = _matmul(x2, w_in[:, d_inner + conv_ch:], passes=3)
    xbc = _conv_silu(main.reshape(b, l, -1), d_inner, conv_ch, conv_w.astype(F32),
                     conv_b.reshape(1, -1).astype(F32))
    heads = np.arange(M2_GROUPS * M2_HEADS_PER_GROUP)
    pf = _placement(heads, heads, M2_GROUPS, M2_HEADS_PER_GROUP)
    pb = _placement(32 + heads, 32 + heads, M2_GROUPS, M2_HEADS_PER_GROUP)
    col_f, col_b = _gate_prep(dt, _pad64(dt_bias), _pad64(a_log), pf, pb, aux_is_sigmoid=False)
    lanes = np.arange(M2_HEADS_PER_GROUP * M2_HEAD_DIM)
    e8 = jnp.asarray(np.arange(M2_HEADS_PER_GROUP)[:, None] == lanes[None, :] // M2_HEAD_DIM, BF16)
    y_f = _ssd_scan(xbc, col_f.reshape(b, l, -1), _row_form(col_f, b, l), e8, rev=False)
    y_b = _ssd_scan(xbc, col_b.reshape(b, l, -1), _row_form(col_b, b, l), e8, rev=True)
    d_full = jnp.repeat(d_skip.astype(F32), M2_HEAD_DIM).reshape(1, -1)
    return _out_ln(y_f.reshape(t, -1), y_b.reshape(t, -1), main, 0,
                   norm_w.reshape(1, -1).astype(F32), w_out.astype(BF16), x_res, ln_g, ln_b,
                   group=M2_HEADS_PER_GROUP * M2_HEAD_DIM, gate_first=True,
                   skip=(xbc.reshape(t, -1), d_full))


def _hgrn2_mixer(x2, b, l, w_in, lb, norm_w, w_out, x_res, ln_g, ln_b):
    t = b * l
    proj = _matmul(x2, w_in.astype(BF16))
    lb = lb.reshape(1, -1).astype(F32)
    lbp = jnp.concatenate([1.0 - lb, jnp.log(lb), jnp.log1p(-lb), jnp.zeros((5, lb.shape[1]), F32)], axis=0)
    p3 = proj.reshape(b, l, -1)
    o_f = _hg_scan(p3, lbp, rev=False)
    o_b = _hg_scan(p3, lbp, rev=True)
    nw = jnp.tile(norm_w.astype(F32), HG_HEADS).reshape(1, -1)
    return _out_ln(o_f.reshape(t, -1), o_b.reshape(t, -1), proj, 4, nw, w_out.astype(BF16),
                   x_res, ln_g, ln_b, group=HG_DIM, gate_first=False)


def _gla_mixer(x2, b, l, w_in, w_gk, b_gk, norm_w, w_out, x_res, ln_g, ln_b):
    t = b * l
    main = _matmul(x2, w_in[:, :3072].astype(BF16))
    r = _matmul(x2, w_in[:, 3072:], passes=3)
    p3 = main.reshape(b, l, -1)
    r3 = r.reshape(b, l, -1)
    rank = w_gk.shape[1]
    zeros = jnp.zeros_like(w_gk[0])
    w_f = jnp.concatenate([w_gk[0], zeros], axis=0).astype(F32)
    w_b = jnp.concatenate([zeros, w_gk[1]], axis=0).astype(F32)
    del rank
    o_f = _gla_scan(p3, r3, w_f, b_gk[0].reshape(1, -1).astype(F32), rev=False)
    o_b = _gla_scan(p3, r3, w_b, b_gk[1].reshape(1, -1).astype(F32), rev=True)
    nw = jnp.tile(norm_w.astype(F32), GLA_HEADS).reshape(1, -1)
    return _out_ln(o_f.reshape(t, -1), o_b.reshape(t, -1), main, 2, nw, w_out.astype(BF16),
                   x_res, ln_g, ln_b, group=GLA_V, gate_first=False)


def _hier_moe_parts(x1, w_group, b_group, w_router, b_router, w_gate, w_up, w_down):
    d = x1.shape[1]
    wt = jnp.zeros((128, d), F32)
    wt = wt.at[0:MOE_GROUPS].set(w_group.T.astype(F32))
    wt = wt.at[8:8 + MOE_GROUPS * MOE_EPG].set(w_router.T.astype(F32))
    bias = jnp.zeros((128, 1), F32)
    bias = bias.at[0:MOE_GROUPS, 0].set(b_group.astype(F32))
    bias = bias.at[8:8 + MOE_GROUPS * MOE_EPG, 0].set(b_router.astype(F32))
    route_row, route_col, cnt = _router(x1, wt, bias)
    counts = cnt[:, :MOE_GROUPS, 0].astype(jnp.int32).reshape(-1)
    return _moe(x1, route_row, route_col, counts,
                w_gate.astype(BF16), w_up.astype(BF16), w_down.astype(BF16))


def kernel(x, p, gdn_w_in, gdn_conv_w, gdn_a_log, gdn_dt_bias, gdn_norm_w, gdn_w_out, m2_w_in, m2_conv_w, m2_conv_b, m2_a_log, m2_dt_bias, m2_d, m2_norm_w, m2_w_out, hg_w_in, hg_lb_logits, hg_norm_w, hg_w_out, gla_w_in, gla_w_gk, gla_b_gk, gla_norm_w, gla_w_out, ln_g, ln_b, moe_w_group, moe_b_group, moe_w_router, moe_b_router, moe_w_gate, moe_w_up, moe_w_down, pe_w_gate, pe_w_proj):
    b, l, d = x.shape
    t = b * l
    depth = p.shape[0]
    lb_all = jnp.cumsum(jax.nn.softmax(hg_lb_logits.astype(F32), axis=0), axis=0)
    lb_all = lb_all - lb_all[0]
    xc = x.reshape(t, d).astype(F32)
    for i in range(depth):
        m, j = i % 4, i // 4
        g1 = ln_g[i, 0].reshape(1, d).astype(F32)
        b1 = ln_b[i, 0].reshape(1, d).astype(F32)
        if m == 0:
            x1 = _gdn_mixer(xc, b, l, gdn_w_in[j], gdn_conv_w[j], gdn_a_log[j], gdn_dt_bias[j],
                            gdn_norm_w[j], gdn_w_out[j], xc, g1, b1)
        elif m == 1:
            x1 = _mamba2_mixer(xc, b, l, m2_w_in[j], m2_conv_w[j], m2_conv_b[j], m2_a_log[j],
                               m2_dt_bias[j], m2_d[j], m2_norm_w[j], m2_w_out[j], xc, g1, b1)
        elif m == 2:
            x1 = _hgrn2_mixer(xc, b, l, hg_w_in[j], lb_all[i], hg_norm_w[j], hg_w_out[j], xc, g1, b1)
        else:
            x1 = _gla_mixer(xc, b, l, gla_w_in[j], gla_w_gk[j], gla_b_gk[j], gla_norm_w[j],
                            gla_w_out[j], xc, g1, b1)
        parts = _hier_moe_parts(x1, moe_w_group[i], moe_b_group[i], moe_w_router[i], moe_b_router[i],
                                moe_w_gate[i], moe_w_up[i], moe_w_down[i])
        xc = _ln_pe(x1, parts, p[i].reshape(t, -1).astype(F32),
                    ln_g[i, 1].reshape(1, d).astype(F32), ln_b[i, 1].reshape(1, d).astype(F32),
                    pe_w_gate[i].astype(BF16), pe_w_proj[i].astype(BF16))
    return xc.reshape(b, l, d).astype(x.dtype)
```

```python
import functools
import math

import numpy as np
import jax
import jax.numpy as jnp
from jax import lax
from jax.experimental import pallas as pl
from jax.experimental.pallas import tpu as pltpu

F32 = jnp.float32
BF16 = jnp.bfloat16

CHUNK = 64
CONV_K = 5
LN_EPS = 1e-5
NORM_EPS = 1e-6
NEG_BIG = -1e30
DEPTH = 4
DEEPNORM_ALPHA = (2 * DEPTH) ** 0.25
GLA_GATE_TEMP = 16.0

VMEM_LIMIT_BYTES = 56 * 1024 * 1024

GDN_HEADS = 16
GDN_HEAD_DIM = 128
M2_GROUPS = 4
M2_HEADS_PER_GROUP = 8
M2_HEAD_DIM = 64
M2_STATE = 128
MOE_GROUPS = 4
MOE_EPG = 8
MOE_CAP = 320
MOE_TILE = 1024
ROUTE_ROWS = 16


def _cparams(sem):
    return pltpu.CompilerParams(dimension_semantics=sem, vmem_limit_bytes=VMEM_LIMIT_BYTES)


_NN = (((1,), (0,)), ((), ()))
_NT = (((1,), (1,)), ((), ()))
_TN = (((0,), (0,)), ((), ()))


def _dot(a, b, dims=_NN):
    return lax.dot_general(a.astype(BF16), b.astype(BF16), dims, preferred_element_type=F32)


def _split3(a):
    hi = a.astype(BF16)
    r1 = a - hi.astype(F32)
    mid = r1.astype(BF16)
    lo = (r1 - mid.astype(F32)).astype(BF16)
    return hi, mid, lo


def _dot_sel_l(sel, b, dims=_NN):
    s = sel.astype(BF16)
    out = None
    for part in _split3(b):
        t = lax.dot_general(s, part, dims, preferred_element_type=F32)
        out = t if out is None else out + t
    return out


def _dot_sel_r(a, sel, dims=_NN):
    s = sel.astype(BF16)
    out = None
    for part in _split3(a):
        t = lax.dot_general(part, s, dims, preferred_element_type=F32)
        out = t if out is None else out + t
    return out


def _dot_hi(a, b, dims=_NN):
    ah = a.astype(BF16)
    al = (a - ah.astype(F32)).astype(BF16)
    bh = b.astype(BF16)
    bl = (b - bh.astype(F32)).astype(BF16)
    out = lax.dot_general(ah, bh, dims, preferred_element_type=F32)
    out = out + lax.dot_general(ah, bl, dims, preferred_element_type=F32)
    out = out + lax.dot_general(al, bh, dims, preferred_element_type=F32)
    return out


def _silu(t):
    return t * jax.nn.sigmoid(t)


def _softplus(t):
    return jnp.maximum(t, 0.0) + jnp.log1p(jnp.exp(-jnp.abs(t)))


def _log_sigmoid(t):
    return jnp.minimum(t, 0.0) - jnp.log1p(jnp.exp(-jnp.abs(t)))


def _iota2(shape, axis):
    return lax.broadcasted_iota(jnp.int32, shape, axis)


def _mm_kernel(x_ref, w_ref, o_ref, *, passes):
    if passes == 1:
        o_ref[...] = jnp.dot(x_ref[...].astype(BF16), w_ref[...].astype(BF16),
                             preferred_element_type=F32)
    else:
        o_ref[...] = _dot_hi(x_ref[...], w_ref[...])


def _matmul(x, w, *, passes=1, tm=1024):
    t, k = x.shape
    n = w.shape[1]
    tm = min(tm, t)
    tn = n if n <= 1024 else (1024 if n % 1024 == 0 else 512)
    assert t % tm == 0 and n % tn == 0
    return pl.pallas_call(
        functools.partial(_mm_kernel, passes=passes),
        grid=(t // tm, n // tn),
        in_specs=[pl.BlockSpec((tm, k), lambda i, j: (i, 0)),
                  pl.BlockSpec((k, tn), lambda i, j: (0, j))],
        out_specs=pl.BlockSpec((tm, tn), lambda i, j: (i, j)),
        out_shape=jax.ShapeDtypeStruct((t, n), F32),
        compiler_params=_cparams(("parallel", "parallel")),
        name="proj_matmul",
    )(x, w)


def _conv_kernel(prev_ref, main_ref, next_ref, w_ref, b_ref, o_ref, buf_ref, *,
                 tl, n_norm_blocks, n_scale_blocks, head_dim, scale):
    i = pl.program_id(1)
    j = pl.program_id(2)
    last = pl.num_programs(1) - 1
    keep_prev = jnp.where(i == 0, 0.0, 1.0)
    keep_next = jnp.where(i == last, 0.0, 1.0)
    buf_ref[0:8, :] = prev_ref[0, 0] * keep_prev
    buf_ref[8:8 + tl, :] = main_ref[0]
    buf_ref[8 + tl:16 + tl, :] = next_ref[0, 0] * keep_next
    w = w_ref[...]
    acc = None
    for d in range(CONV_K):
        term = buf_ref[6 + d:6 + d + tl, :] * w[d:d + 1, :]
        acc = term if acc is None else acc + term
    y = _silu(acc + b_ref[...])
    if n_norm_blocks > 0:
        do_norm = j < n_norm_blocks
        fac = jnp.where(j < n_scale_blocks, scale, 1.0)
        pieces = []
        for h in range(y.shape[1] // head_dim):
            yh = y[:, h * head_dim:(h + 1) * head_dim]
            ss = jnp.sum(yh * yh, axis=1, keepdims=True)
            r = jnp.where(do_norm, lax.rsqrt(ss + NORM_EPS) * fac, 1.0)
            pieces.append(yh * r)
        y = jnp.concatenate(pieces, axis=1)
    o_ref[0] = y


def _conv_silu(proj, lane0, width, w, bias, *, n_norm_blocks=0, n_scale_blocks=0,
               head_dim=128, scale=1.0, tc=512):
    b, l, n = proj.shape
    tl = min(512, l)
    assert l % tl == 0 and width % tc == 0 and lane0 % tc == 0 and tl % 8 == 0
    off = lane0 // tc
    halo = proj.reshape(b, l // 8, 8, n)
    nl8 = l // 8
    step8 = tl // 8
    return pl.pallas_call(
        functools.partial(_conv_kernel, tl=tl, n_norm_blocks=n_norm_blocks,
                          n_scale_blocks=n_scale_blocks, head_dim=head_dim, scale=scale),
        grid=(b, l // tl, width // tc),
        in_specs=[
            pl.BlockSpec((1, 1, 8, tc), lambda bi, i, j: (bi, jnp.maximum(i * step8 - 1, 0), 0, off + j)),
            pl.BlockSpec((1, tl, tc), lambda bi, i, j: (bi, i, off + j)),
            pl.BlockSpec((1, 1, 8, tc), lambda bi, i, j: (bi, jnp.minimum((i + 1) * step8, nl8 - 1), 0, off + j)),
            pl.BlockSpec((CONV_K, tc), lambda bi, i, j: (0, j)),
            pl.BlockSpec((1, tc), lambda bi, i, j: (0, j)),
        ],
        out_specs=pl.BlockSpec((1, tl, tc), lambda bi, i, j: (bi, i, j)),
        out_shape=jax.ShapeDtypeStruct((b, l, width), F32),
        scratch_shapes=[pltpu.VMEM((tl + 16, tc), F32)],
        compiler_params=_cparams(("parallel", "parallel", "parallel")),
        name="conv_silu",
    )(halo, proj, halo, w, bias)


GATE_ROWS = 512


def _gate_prep_kernel(raw_ref, bias_ref, alog_ref, tri_l_ref, tri_u_ref, ones_ref,
                      pf_ref, pb_ref, of_ref, ob_ref, *, aux_is_sigmoid):
    raw = raw_ref[...]
    sp = _softplus(raw + bias_ref[...])
    g = -jnp.exp(alog_ref[...]) * sp
    aux = jax.nn.sigmoid(raw) if aux_is_sigmoid else sp
    gc_l = _dot_sel_l(tri_l_ref[...], g)
    gc_u = _dot_sel_l(tri_u_ref[...], g)
    gtot = _dot_sel_l(ones_ref[...], g)
    pf = pf_ref[...]
    pb = pb_ref[...]
    of_ref[...] = (_dot_sel_r(gc_l, pf[0]) + _dot_sel_r(aux, pf[1]) + _dot_sel_r(gtot, pf[2]))
    ob_ref[...] = (_dot_sel_r(gc_u, pb[0]) + _dot_sel_r(aux, pb[1]) + _dot_sel_r(gtot, pb[2]))


def _chunk_masks(rows):
    r = np.arange(rows)
    same = (r[:, None] // CHUNK) == (r[None, :] // CHUNK)
    tri_l = same & (r[None, :] <= r[:, None])
    tri_u = same & (r[None, :] >= r[:, None])
    return (jnp.asarray(tri_l, BF16), jnp.asarray(tri_u, BF16), jnp.asarray(same, BF16))


def _placement(src_gc, src_aux, n_slab, per_slab):
    p = np.zeros((3, 64, n_slab * 128), np.float32)
    for h in range(n_slab * per_slab):
        base = (h // per_slab) * 128 + h % per_slab
        p[0, src_gc[h], base] = 1.0
        p[1, src_aux[h], base + 32] = 1.0
        p[2, src_gc[h], base + 64] = 1.0
    return jnp.asarray(p, BF16)


def _gate_prep(raw, bias64, alog64, pf, pb, *, aux_is_sigmoid):
    t = raw.shape[0]
    rows = min(GATE_ROWS, t)
    w = pf.shape[2]
    tri_l, tri_u, ones = _chunk_masks(rows)
    full = lambda shape: pl.BlockSpec(shape, lambda i: (0,) * len(shape))
    return pl.pallas_call(
        functools.partial(_gate_prep_kernel, aux_is_sigmoid=aux_is_sigmoid),
        grid=(t // rows,),
        in_specs=[pl.BlockSpec((rows, 64), lambda i: (i, 0)),
                  full((1, 64)), full((1, 64)),
                  full((rows, rows)), full((rows, rows)), full((rows, rows)),
                  full((3, 64, w)), full((3, 64, w))],
        out_specs=[pl.BlockSpec((rows, w), lambda i: (i, 0)),
                   pl.BlockSpec((rows, w), lambda i: (i, 0))],
        out_shape=[jax.ShapeDtypeStruct((t, w), F32), jax.ShapeDtypeStruct((t, w), F32)],
        compiler_params=_cparams(("parallel",)),
        name="gate_prep",
    )(raw, bias64, alog64, tri_l, tri_u, ones, pf, pb)


def _row_form(col, b, l):
    w = col.shape[1]
    return jnp.swapaxes(col.reshape(b, l // CHUNK, CHUNK, w), 2, 3)


def _tri_masks(rev):
    ri = _iota2((CHUNK, CHUNK), 0)
    ci = _iota2((CHUNK, CHUNK), 1)
    if rev:
        return ri, ci, ci >= ri, ci > ri
    return ri, ci, ci <= ri, ci < ri


GDN_SLAB = 4
GDN_SLAB_ROWS = GDN_SLAB * CHUNK


GDN_LEVELS = int(math.log2(CHUNK))
GDN_MASK_EYE, GDN_MASK_INCL, GDN_MASK_STRICT, GDN_MASK_LEVEL0 = 0, 1, 2, 3


def _gdn_masks(rev):
    n = GDN_SLAB_ROWS
    r = np.arange(n)[:, None]
    c = np.arange(n)[None, :]
    same_head = (r // CHUNK) == (c // CHUNK)
    out = [r == c, same_head & ((c >= r) if rev else (c <= r)), same_head & ((c > r) if rev else (c < r))]
    s = 1
    while s < CHUNK:
        same = (r // (2 * s)) == (c // (2 * s))
        r_hi = (r % (2 * s)) >= s
        c_hi = (c % (2 * s)) >= s
        out.append(same & ~r_hi & c_hi if rev else same & r_hi & ~c_hi)
        s *= 2
    return jnp.asarray(np.stack(out), F32)


def _gdn_scan_kernel(q_ref, k_ref, v_ref, gcol_ref, grow_ref, m_ref, o_ref, s_ref):
    @pl.when(pl.program_id(1) == 0)
    def _():
        s_ref[...] = jnp.zeros_like(s_ref)

    c = CHUNK
    d = GDN_HEAD_DIM
    first = _iota2((c, 2 * d), 1) < d
    pair_rows = _iota2((2 * c, 2 * d), 0) < c
    pair_lanes = _iota2((2 * c, 2 * d), 1) < d
    vn_mask = pair_rows == pair_lanes
    state_mask = (_iota2((2 * d, 2 * d), 0) < d) == (_iota2((2 * d, 2 * d), 1) < d)
    slabs = range(GDN_HEADS // GDN_SLAB)
    heads = range(GDN_SLAB)

    gcols, q_sts, k_sts, ps, abs_, ts, rhss = [], [], [], [], [], [], []
    for j in slabs:
        gcol = gcol_ref[0][:, 128 * j:128 * (j + 1)]
        grow = grow_ref[0, 0][j:j + 1, :]
        qk_head = [(GDN_SLAB * j + i) // 2 for i in heads]
        q_st = jnp.concatenate([q_ref[0][:, d * h:d * (h + 1)] for h in qk_head], axis=0)
        k_st = jnp.concatenate([k_ref[0][:, d * h:d * (h + 1)] for h in qk_head], axis=0)
        v_st = jnp.concatenate([v_ref[0][:, d * (GDN_SLAB * j + i):d * (GDN_SLAB * j + i + 1)]
                                for i in heads], axis=0)
        gc_c = jnp.concatenate([gcol[:, i:i + 1] for i in heads], axis=0)
        beta_c = jnp.concatenate([gcol[:, 32 + i:33 + i] for i in heads], axis=0)
        decay = jnp.exp(jnp.minimum(gc_c - grow, 0.0)) * m_ref[GDN_MASK_INCL]
        kb = k_st.astype(BF16)
        kk = lax.dot_general(kb, kb, _NT, preferred_element_type=F32)
        ps.append(lax.dot_general(q_st.astype(BF16), kb, _NT, preferred_element_type=F32) * decay)
        a = kk * decay * m_ref[GDN_MASK_STRICT] * beta_c
        abs_.append(a.astype(BF16))
        ts.append(m_ref[GDN_MASK_EYE] - a * m_ref[GDN_MASK_LEVEL0])
        rhss.append(jnp.concatenate([v_st * beta_c, k_st * (beta_c * jnp.exp(gc_c))], axis=1))
        gcols.append(gcol)
        q_sts.append(q_st)
        k_sts.append(k_st)

    for lvl in range(1, GDN_LEVELS):
        tbs = [t.astype(BF16) for t in ts]
        xs = [jnp.dot(tb, ab, preferred_element_type=F32).astype(BF16) for tb, ab in zip(tbs, abs_)]
        ts = [t - jnp.dot(x, tb, preferred_element_type=F32) * m_ref[GDN_MASK_LEVEL0 + lvl]
              for t, x, tb in zip(ts, xs, tbs)]
    uws = [_dot(t, rhs) for t, rhs in zip(ts, rhss)]

    for j in slabs:
        gcol, q_st, k_st, p, uw = gcols[j], q_sts[j], k_sts[j], ps[j], uws[j]
        for pr in range(GDN_SLAB // 2):
            ha, hb = 2 * pr, 2 * pr + 1
            ra = slice(c * ha, c * (ha + 1))
            rb = slice(c * hb, c * (hb + 1))
            u_pair = jnp.concatenate([uw[ra, :d], uw[rb, :d]], axis=1)
            w_pair = jnp.concatenate([uw[ra, d:], uw[rb, d:]], axis=1)
            q_pair = jnp.concatenate([q_st[ra], q_st[rb]], axis=1)
            k_pair = jnp.concatenate([k_st[ra], k_st[rb]], axis=1)
            gc_pair = jnp.where(first, gcol[:, ha:ha + 1], gcol[:, hb:hb + 1])
            gt_pair = jnp.where(first, gcol[:, 64 + ha:65 + ha], gcol[:, 64 + hb:65 + hb])
            idx = (GDN_SLAB // 2) * j + pr
            s = s_ref[idx]
            wq = _dot(jnp.concatenate([w_pair, q_pair * jnp.exp(gc_pair)], axis=0), s)
            v_new = u_pair - wq[:c]
            p_pair = p[ra, 2 * c * pr:2 * c * (pr + 1)] + p[rb, 2 * c * pr:2 * c * (pr + 1)]
            vn_bd = jnp.where(vn_mask, jnp.concatenate([v_new, v_new], axis=0), 0.0)
            o_ref[0, :, 2 * d * idx:2 * d * (idx + 1)] = wq[c:] + _dot(p_pair, vn_bd)
            k_dec = k_pair * jnp.exp(gt_pair - gc_pair)
            s_ref[idx] = (s * jnp.exp(gt_pair[0:1, :])
                          + jnp.where(state_mask, _dot(k_dec, v_new, _TN), 0.0))


def _gdn_row_form(col, b, l):
    nslab = GDN_HEADS // GDN_SLAB
    g = col.reshape(b, l // CHUNK, CHUNK, nslab, 128)[..., :GDN_SLAB]
    g = jnp.transpose(g, (0, 1, 3, 4, 2)).reshape(b, l // CHUNK, nslab, GDN_SLAB_ROWS)
    return jnp.pad(g, ((0, 0), (0, 0), (0, 8 - nslab), (0, 0)))


def _gdn_scan(qkv, col, row, *, rev):
    b, l, _ = qkv.shape
    nc = l // CHUNK
    kw = GDN_HEADS // 2 * GDN_HEAD_DIM
    vw = GDN_HEADS * GDN_HEAD_DIM
    cidx = (lambda n: nc - 1 - n) if rev else (lambda n: n)
    masks = _gdn_masks(rev)
    return pl.pallas_call(
        _gdn_scan_kernel,
        grid=(b, nc),
        in_specs=[
            pl.BlockSpec((1, CHUNK, kw), lambda bi, n: (bi, cidx(n), 0)),
            pl.BlockSpec((1, CHUNK, kw), lambda bi, n: (bi, cidx(n), 1)),
            pl.BlockSpec((1, CHUNK, vw), lambda bi, n: (bi, cidx(n), 1)),
            pl.BlockSpec((1, CHUNK, col.shape[2]), lambda bi, n: (bi, cidx(n), 0)),
            pl.BlockSpec((1, 1, 8, GDN_SLAB_ROWS), lambda bi, n: (bi, cidx(n), 0, 0)),
            pl.BlockSpec(masks.shape, lambda bi, n: (0, 0, 0)),
        ],
        out_specs=pl.BlockSpec((1, CHUNK, vw), lambda bi, n: (bi, cidx(n), 0)),
        out_shape=jax.ShapeDtypeStruct((b, l, vw), F32),
        scratch_shapes=[pltpu.VMEM((GDN_HEADS // 2, 2 * GDN_HEAD_DIM, 2 * GDN_HEAD_DIM), F32)],
        compiler_params=_cparams(("parallel", "arbitrary")),
        name="gdn_scan_bwd" if rev else "gdn_scan_fwd",
    )(qkv, qkv, qkv, col, row, masks)


def _ssd_scan_kernel(c_ref, b_ref, x_ref, gcol_ref, grow_ref, e8_ref, y_ref, s_ref, *, rev):
    @pl.when(pl.program_id(1) == 0)
    def _():
        s_ref[...] = jnp.zeros_like(s_ref)

    _, _, incl, _ = _tri_masks(rev)
    e8 = e8_ref[...]
    r_heads = M2_HEADS_PER_GROUP
    gw = r_heads * M2_HEAD_DIM
    lane_head = _iota2((CHUNK, gw), 1) // M2_HEAD_DIM
    groups = range(M2_GROUPS)
    ccs = [c_ref[0][:, M2_STATE * g:M2_STATE * (g + 1)] for g in groups]
    bbs = [b_ref[0][:, M2_STATE * g:M2_STATE * (g + 1)] for g in groups]
    gcols = [gcol_ref[0][:, 128 * g:128 * (g + 1)] for g in groups]
    grows = [grow_ref[0, 0][128 * g:128 * (g + 1), :] for g in groups]
    gc_xs = [_dot_sel_r(gcols[g][:, 0:r_heads], e8) for g in groups]
    dt_xs = [_dot_sel_r(gcols[g][:, 32:32 + r_heads], e8) for g in groups]
    gt_xs = [_dot_sel_r(gcols[g][:, 64:64 + r_heads], e8) for g in groups]
    us = [x_ref[0][:, gw * g:gw * (g + 1)] * dt_xs[g] for g in groups]
    scores = [_dot(ccs[g], bbs[g], _NT) for g in groups]
    states = [s_ref[g] for g in groups]
    ys = [jnp.exp(gc_xs[g]) * _dot(ccs[g], states[g]) for g in groups]
    for r in range(r_heads):
        for g in groups:
            decay = jnp.exp(jnp.where(incl, gcols[g][:, r:r + 1] - grows[g][r:r + 1, :], NEG_BIG))
            ys[g] = ys[g] + _dot(scores[g] * decay, jnp.where(lane_head == r, us[g], 0.0))
    for g in groups:
        s_ref[g] = (states[g] * jnp.exp(gt_xs[g][0:1, :])
                    + _dot(bbs[g], us[g] * jnp.exp(gt_xs[g] - gc_xs[g]), _TN))
        y_ref[0, :, gw * g:gw * (g + 1)] = ys[g]


def _ssd_scan(xbc, col, row, e8, *, rev):
    b, l, _ = xbc.shape
    nc = l // CHUNK
    gw = M2_HEADS_PER_GROUP * M2_HEAD_DIM
    xw = M2_GROUPS * gw
    sw = M2_GROUPS * M2_STATE
    cidx = (lambda n: nc - 1 - n) if rev else (lambda n: n)
    return pl.pallas_call(
        functools.partial(_ssd_scan_kernel, rev=rev),
        grid=(b, nc),
        in_specs=[
            pl.BlockSpec((1, CHUNK, sw), lambda bi, n: (bi, cidx(n), xw // sw + 1)),
            pl.BlockSpec((1, CHUNK, sw), lambda bi, n: (bi, cidx(n), xw // sw)),
            pl.BlockSpec((1, CHUNK, xw), lambda bi, n: (bi, cidx(n), 0)),
            pl.BlockSpec((1, CHUNK, M2_GROUPS * 128), lambda bi, n: (bi, cidx(n), 0)),
            pl.BlockSpec((1, 1, M2_GROUPS * 128, CHUNK), lambda bi, n: (bi, cidx(n), 0, 0)),
            pl.BlockSpec((M2_HEADS_PER_GROUP, gw), lambda bi, n: (0, 0)),
        ],
        out_specs=pl.BlockSpec((1, CHUNK, xw), lambda bi, n: (bi, cidx(n), 0)),
        out_shape=jax.ShapeDtypeStruct((b, l, xw), F32),
        scratch_shapes=[pltpu.VMEM((M2_GROUPS, M2_STATE, gw), F32)],
        compiler_params=_cparams(("parallel", "arbitrary")),
        name="ssd_scan_bwd" if rev else "ssd_scan_fwd",
    )(xbc, xbc, xbc, col, row, e8)


def _gla_chunk(qs_, ks_, vs_, gs_, sts_, rev):
    c = CHUNK
    nh = len(qs_)
    hr = range(nh)
    ri, ci, incl, _ = _tri_masks(rev)
    tri = jnp.where(incl, 1.0, 0.0)
    gsums = [_dot_sel_l(tri, gs_[h]) for h in hr]
    gtots = [gs[0:1] if rev else gs[c - 1:c] for gs in gsums]
    os_ = [_dot(qs_[h] * jnp.exp(gsums[h]), sts_[h], _NT) for h in hr]
    st_new = [sts_[h] * jnp.exp(gtots[h]) + _dot(vs_[h], ks_[h] * jnp.exp(gtots[h] - gsums[h]), _TN)
              for h in hr]

    row = _iota2((c, 1), 0)
    attns = [jnp.zeros((c, c), F32) for _ in hr]
    s = c // 2
    while s >= 8:
        upper_half = (row & (2 * s - 1)) >= s
        q_rows = (~upper_half) if rev else upper_half
        shift = int(math.log2(2 * s))
        same_pair = (ri >> shift) == (ci >> shift)
        prods = []
        for h in hr:
            gsum = gsums[h]
            pieces = []
            for p in range(c // (2 * s)):
                r = p * 2 * s + (s if rev else s - 1)
                pieces.append(jnp.broadcast_to(gsum[r:r + 1], (2 * s, gsum.shape[1])))
            ref = pieces[0] if len(pieces) == 1 else jnp.concatenate(pieces, axis=0)
            e = jnp.exp(-jnp.abs(gsum - ref))
            q_s = jnp.where(q_rows, qs_[h] * e, 0.0)
            k_s = jnp.where(q_rows, 0.0, ks_[h] * e)
            prods.append(_dot(q_s, k_s, _NT))
        attns = [attns[h] + jnp.where(same_pair, prods[h], 0.0) for h in hr]
        s //= 2

    lane = _iota2((8, c), 1)
    sub = _iota2((8, 1), 0)
    blocks = [[] for _ in hr]
    for blk in range(c // 8):
        rows = slice(8 * blk, 8 * blk + 8)
        accs = [jnp.zeros((8, c), F32) for _ in hr]
        for j in range(8):
            ok = (lane == 8 * blk + j) & ((sub <= j) if rev else (sub >= j))
            for h in hr:
                gb = gsums[h][rows]
                e = jnp.exp(jnp.minimum(gb - gb[j:j + 1], 0.0))
                red = jnp.sum(qs_[h][rows] * ks_[h][rows][j:j + 1] * e, axis=1, keepdims=True)
                accs[h] = jnp.where(ok, red, accs[h])
        for h in hr:
            blocks[h].append(accs[h])
    os_ = [os_[h] + _dot(attns[h] + jnp.concatenate(blocks[h], axis=0), vs_[h]) for h in hr]
    return os_, st_new


HG_HB = 8
HG_HEADS = 8
HG_DIM = 128


def _hg_scan_kernel(q_ref, f_ref, v_ref, lb_ref, o_ref, st_ref, *, hb, rev):
    @pl.when(pl.program_id(2) == 0)
    def _():
        st_ref[...] = jnp.zeros_like(st_ref)

    lbp = lb_ref[...]
    d = HG_DIM
    qs, ks, vs, gs = [], [], [], []
    for i in range(hb):
        sl = slice(i * d, (i + 1) * d)
        f = f_ref[0][:, sl]
        a = lbp[1:2, sl]
        bterm = lbp[2:3, sl] + _log_sigmoid(f)
        qs.append(_silu(q_ref[0][:, sl]))
        ks.append(lbp[0:1, sl] * jax.nn.sigmoid(-f))
        vs.append(v_ref[0][:, sl])
        gs.append(jnp.maximum(a, bterm) + jnp.log1p(jnp.exp(-jnp.abs(a - bterm))))
    outs, st_new = _gla_chunk(qs, ks, vs, gs, [st_ref[i] for i in range(hb)], rev)
    for i in range(hb):
        st_ref[i] = st_new[i]
    o_ref[0] = jnp.concatenate(outs, axis=1)


def _hg_scan(proj, lbp, *, rev):
    b, l, _ = proj.shape
    nc = l // CHUNK
    hb = HG_HB
    nsl = HG_HEADS // hb
    f_off = (2 if rev else 1) * nsl
    cidx = (lambda n: nc - 1 - n) if rev else (lambda n: n)
    blk = (1, CHUNK, hb * HG_DIM)
    return pl.pallas_call(
        functools.partial(_hg_scan_kernel, hb=hb, rev=rev),
        grid=(b, nsl, nc),
        in_specs=[
            pl.BlockSpec(blk, lambda bi, j, n: (bi, cidx(n), j)),
            pl.BlockSpec(blk, lambda bi, j, n: (bi, cidx(n), f_off + j)),
            pl.BlockSpec(blk, lambda bi, j, n: (bi, cidx(n), 3 * nsl + j)),
            pl.BlockSpec((8, hb * HG_DIM), lambda bi, j, n: (0, j)),
        ],
        out_specs=pl.BlockSpec(blk, lambda bi, j, n: (bi, cidx(n), j)),
        out_shape=jax.ShapeDtypeStruct((b, l, HG_HEADS * HG_DIM), F32),
        scratch_shapes=[pltpu.VMEM((hb, HG_DIM, HG_DIM), F32)],
        compiler_params=_cparams(("parallel", "parallel", "arbitrary")),
        name="hgrn2_scan_bwd" if rev else "hgrn2_scan_fwd",
    )(proj, proj, proj, lbp)


GLA_HB = 4
GLA_HEADS = 4
GLA_K = 128
GLA_V = 256


def _gla_scan_kernel(q_ref, k_ref, v_ref, r_ref, wgk_ref, bgk_ref, o_ref, st_ref, *, hb, rev):
    @pl.when(pl.program_id(2) == 0)
    def _():
        st_ref[...] = jnp.zeros_like(st_ref)

    pre = _dot_hi(r_ref[0], wgk_ref[...]) + bgk_ref[...]
    gk = _log_sigmoid(pre) * (1.0 / GLA_GATE_TEMP)
    sls = [slice(i * GLA_K, (i + 1) * GLA_K) for i in range(hb)]
    outs, st_new = _gla_chunk([q_ref[0][:, sl] * (GLA_K ** -0.5) for sl in sls],
                              [k_ref[0][:, sl] for sl in sls],
                              [v_ref[0][:, i * GLA_V:(i + 1) * GLA_V] for i in range(hb)],
                              [gk[:, sl] for sl in sls],
                              [st_ref[i] for i in range(hb)], rev)
    for i in range(hb):
        st_ref[i] = st_new[i]
    o_ref[0] = jnp.concatenate(outs, axis=1)


def _gla_scan(proj, r, wgk, bgk, *, rev):
    b, l, _ = proj.shape
    nc = l // CHUNK
    hb = GLA_HB
    nsl = GLA_HEADS // hb
    cidx = (lambda n: nc - 1 - n) if rev else (lambda n: n)
    return pl.pallas_call(
        functools.partial(_gla_scan_kernel, hb=hb, rev=rev),
        grid=(b, nsl, nc),
        in_specs=[
            pl.BlockSpec((1, CHUNK, hb * GLA_K), lambda bi, j, n: (bi, cidx(n), j)),
            pl.BlockSpec((1, CHUNK, hb * GLA_K), lambda bi, j, n: (bi, cidx(n), nsl + j)),
            pl.BlockSpec((1, CHUNK, hb * GLA_V), lambda bi, j, n: (bi, cidx(n), nsl + j)),
            pl.BlockSpec((1, CHUNK, r.shape[2]), lambda bi, j, n: (bi, cidx(n), 0)),
            pl.BlockSpec((wgk.shape[0], hb * GLA_K), lambda bi, j, n: (0, j)),
            pl.BlockSpec((1, hb * GLA_K), lambda bi, j, n: (0, j)),
        ],
        out_specs=pl.BlockSpec((1, CHUNK, hb * GLA_V), lambda bi, j, n: (bi, cidx(n), j)),
        out_shape=jax.ShapeDtypeStruct((b, l, GLA_HEADS * GLA_V), F32),
        scratch_shapes=[pltpu.VMEM((hb, GLA_V, GLA_K), F32)],
        compiler_params=_cparams(("parallel", "parallel", "arbitrary")),
        name="gla_scan_bwd" if rev else "gla_scan_fwd",
    )(proj, proj, proj, r, wgk, bgk)


def _layernorm_rows(t, g, b):
    mu = jnp.mean(t, axis=1, keepdims=True)
    tc = t - mu
    var = jnp.mean(tc * tc, axis=1, keepdims=True)
    return tc * lax.rsqrt(var + LN_EPS) * g + b


def _out_ln_kernel(*refs, group, gate_first, has_skip):
    if has_skip:
        of_ref, ob_ref, z_ref, xs_ref, d_ref, nw_ref, w_ref, x_ref, g_ref, b_ref, o_ref = refs
        y = of_ref[...] + ob_ref[...] + d_ref[...] * xs_ref[...]
    else:
        of_ref, ob_ref, z_ref, nw_ref, w_ref, x_ref, g_ref, b_ref, o_ref = refs
        y = of_ref[...] + ob_ref[...]
    gate = _silu(z_ref[...])
    if gate_first:
        y = y * gate
    pieces = []
    for h in range(y.shape[1] // group):
        yh = y[:, h * group:(h + 1) * group]
        ms = jnp.mean(yh * yh, axis=1, keepdims=True)
        pieces.append(yh * lax.rsqrt(ms + NORM_EPS))
    y = jnp.concatenate(pieces, axis=1) * nw_ref[...]
    if not gate_first:
        y = y * gate
    h_out = _dot(y, w_ref[...])
    o_ref[...] = _layernorm_rows(DEEPNORM_ALPHA * x_ref[...] + h_out, g_ref[...], b_ref[...])


def _out_ln(o_f, o_b, z_arr, z_block, nw_full, w_out, x_res, ln_g, ln_b, *, group, gate_first,
            skip=None, tm=256):
    t, din = o_f.shape
    d = w_out.shape[1]
    tm = min(tm, t)
    row = lambda width, blk=0: pl.BlockSpec((tm, width), lambda i: (i, blk))
    full = lambda shape: pl.BlockSpec(shape, lambda i: (0,) * len(shape))
    ins = [o_f, o_b, z_arr]
    specs = [row(din), row(din), row(din, z_block)]
    if skip is not None:
        xs_arr, d_full = skip
        ins += [xs_arr, d_full]
        specs += [row(din), full((1, din))]
    ins += [nw_full, w_out, x_res, ln_g, ln_b]
    specs += [full((1, din)), full((din, d)), row(d), full((1, d)), full((1, d))]
    return pl.pallas_call(
        functools.partial(_out_ln_kernel, group=group, gate_first=gate_first, has_skip=skip is not None),
        grid=(t // tm,),
        in_specs=specs,
        out_specs=row(d),
        out_shape=jax.ShapeDtypeStruct((t, d), F32),
        compiler_params=_cparams(("parallel",)),
        name="gated_norm_out_ln",
    )(*ins)


def _router_kernel(x_ref, wt_ref, bias_ref, tri_ref, row_ref, col_ref, cnt_ref):
    x = x_ref[...]
    tm = x.shape[0]
    logits = _dot_hi(wt_ref[...], x, _NT) + bias_ref[...]
    sub8 = _iota2((8, tm), 0).astype(F32)
    gl = jnp.where(sub8 < MOE_GROUPS, logits[0:8], NEG_BIG)
    gmax = jnp.max(gl, axis=0, keepdims=True)
    gidx = jnp.min(jnp.where(gl == gmax, sub8, 8.0), axis=0, keepdims=True)
    gprob = 1.0 / jnp.sum(jnp.exp(gl - gmax), axis=0, keepdims=True)
    el = logits[8:16]
    for g in range(1, MOE_GROUPS):
        el = jnp.where(gidx == g, logits[8 + 8 * g:16 + 8 * g], el)
    m1 = jnp.max(el, axis=0, keepdims=True)
    i1 = jnp.min(jnp.where(el == m1, sub8, 8.0), axis=0, keepdims=True)
    el2 = jnp.where(sub8 == i1, NEG_BIG, el)
    m2 = jnp.max(el2, axis=0, keepdims=True)
    i2 = jnp.min(jnp.where(el2 == m2, sub8, 8.0), axis=0, keepdims=True)
    e2 = jnp.exp(m2 - m1)
    w1 = gprob / (1.0 + e2)
    w2 = gprob * e2 / (1.0 + e2)
    within = jnp.where(sub8 == i1, w1, 0.0) + jnp.where(sub8 == i2, w2, 0.0)
    onehot = jnp.where(sub8 == gidx, 1.0, 0.0)
    before = _dot(onehot, tri_ref[...])
    pos = jnp.sum(onehot * before, axis=0, keepdims=True)
    rec = jnp.where(sub8 == 0, gidx, jnp.where(sub8 == 1, pos, 0.0))
    rows = jnp.concatenate([within, rec], axis=0)
    row_ref[...] = rows
    padded = jnp.concatenate([rows, jnp.zeros((128 - ROUTE_ROWS, tm), F32)], axis=0)
    col_ref[...] = padded.T
    cnt_ref[0] = jnp.broadcast_to(jnp.sum(onehot, axis=1, keepdims=True), (8, 128))


def _router(x, wt, bias_col):
    t, d = x.shape
    tm = min(MOE_TILE, t)
    nt = t // tm
    r = np.arange(tm)
    tri = jnp.asarray(r[:, None] < r[None, :], BF16)
    full = lambda shape: pl.BlockSpec(shape, lambda i: (0,) * len(shape))
    return pl.pallas_call(
        _router_kernel,
        grid=(nt,),
        in_specs=[pl.BlockSpec((tm, d), lambda i: (i, 0)), full((128, d)), full((128, 1)), full((tm, tm))],
        out_specs=[pl.BlockSpec((ROUTE_ROWS, tm), lambda i: (0, i)),
                   pl.BlockSpec((tm, 128), lambda i: (i, 0)),
                   pl.BlockSpec((1, 8, 128), lambda i: (i, 0, 0))],
        out_shape=[jax.ShapeDtypeStruct((ROUTE_ROWS, t), F32),
                   jax.ShapeDtypeStruct((t, 128), F32),
                   jax.ShapeDtypeStruct((nt, 8, 128), F32)],
        compiler_params=_cparams(("parallel",)),
        name="moe_router",
    )(x, wt, bias_col, tri)


def _moe_kernel(cnt_ref, x_ref, row_ref, col_ref, wg_ref, wu_ref, wd_ref, o_ref, *, cap):
    g = pl.program_id(0)
    ti = pl.program_id(1)
    n = cnt_ref[ti * MOE_GROUPS + g]
    nb = (n + cap - 1) // cap
    tm = x_ref.shape[0]
    gf = g.astype(F32)
    rows = row_ref[...]
    cols = col_ref[...]
    in_g_row = rows[8:9, :] == gf
    pos_row = rows[9:10, :]
    in_g_col = cols[:, 8:9] == gf
    pos_col = cols[:, 9:10]
    xb = x_ref[...].astype(BF16)
    o_ref[0] = jnp.zeros((tm, o_ref.shape[2]), F32)

    def body(bi, carry):
        base = (bi * cap).astype(F32)
        slot_r = _iota2((cap, tm), 0).astype(F32) + base
        sel = jnp.where(in_g_row & (slot_r == pos_row), 1.0, 0.0).astype(BF16)
        xg = jnp.dot(sel, xb, preferred_element_type=F32).astype(BF16)
        cw = _dot_sel_l(sel, cols)
        y = jnp.zeros((cap, o_ref.shape[2]), F32)
        for e in range(MOE_EPG):
            hg = jnp.dot(xg, wg_ref[0, e], preferred_element_type=F32)
            hu = jnp.dot(xg, wu_ref[0, e], preferred_element_type=F32)
            h = _silu(hg) * hu * cw[:, e:e + 1]
            y = y + _dot(h, wd_ref[0, e])
        slot_c = _iota2((tm, cap), 1).astype(F32) + base
        sel_t = jnp.where(in_g_col & (slot_c == pos_col), 1.0, 0.0).astype(BF16)
        o_ref[0] += jnp.dot(sel_t, y.astype(BF16), preferred_element_type=F32)
        return carry

    lax.fori_loop(0, nb, body, 0)


def _moe(x, route_row, route_col, counts, wg, wu, wd):
    t, d = x.shape
    tm = min(MOE_TILE, t)
    cap = min(MOE_CAP, tm)
    nt = t // tm
    hid = wg.shape[3]
    grid_spec = pltpu.PrefetchScalarGridSpec(
        num_scalar_prefetch=1,
        grid=(MOE_GROUPS, nt),
        in_specs=[
            pl.BlockSpec((tm, d), lambda g, i, c: (i, 0)),
            pl.BlockSpec((ROUTE_ROWS, tm), lambda g, i, c: (0, i)),
            pl.BlockSpec((tm, 128), lambda g, i, c: (i, 0)),
            pl.BlockSpec((1, MOE_EPG, d, hid), lambda g, i, c: (g, 0, 0, 0)),
            pl.BlockSpec((1, MOE_EPG, d, hid), lambda g, i, c: (g, 0, 0, 0)),
            pl.BlockSpec((1, MOE_EPG, hid, d), lambda g, i, c: (g, 0, 0, 0)),
        ],
        out_specs=pl.BlockSpec((1, tm, d), lambda g, i, c: (g, i, 0)),
    )
    return pl.pallas_call(
        functools.partial(_moe_kernel, cap=cap),
        grid_spec=grid_spec,
        out_shape=jax.ShapeDtypeStruct((MOE_GROUPS, t, d), F32),
        compiler_params=_cparams(("arbitrary", "arbitrary")),
        name="moe_experts",
    )(counts, x, route_row, route_col, wg, wu, wd)


def _ln_pe_kernel(x_ref, m_ref, p_ref, g_ref, b_ref, wg_ref, wp_ref, o_ref):
    h = m_ref[0]
    for g in range(1, MOE_GROUPS):
        h = h + m_ref[g]
    y = _layernorm_rows(DEEPNORM_ALPHA * x_ref[...] + h, g_ref[...], b_ref[...])
    gate = jax.nn.sigmoid(_dot(y, wg_ref[...]))
    o_ref[...] = y + gate * _dot(p_ref[...], wp_ref[...])


def _ln_pe(x, moe_parts, p, ln_g, ln_b, w_gate, w_proj, *, tm=512):
    t, d = x.shape
    pd = p.shape[1]
    tm = min(tm, t)
    full = lambda shape: pl.BlockSpec(shape, lambda i: (0,) * len(shape))
    return pl.pallas_call(
        _ln_pe_kernel,
        grid=(t // tm,),
        in_specs=[pl.BlockSpec((tm, d), lambda i: (i, 0)),
                  pl.BlockSpec((MOE_GROUPS, tm, d), lambda i: (0, i, 0)),
                  pl.BlockSpec((tm, pd), lambda i: (i, 0)),
                  full((1, d)), full((1, d)), full((d, d)), full((pd, d))],
        out_specs=pl.BlockSpec((tm, d), lambda i: (i, 0)),
        out_shape=jax.ShapeDtypeStruct((t, d), F32),
        compiler_params=_cparams(("parallel",)),
        name="ln_input_embedding",
    )(x, moe_parts, p, ln_g, ln_b, w_gate, w_proj)


def _pad64(v):
    v = v.reshape(1, -1).astype(F32)
    return jnp.pad(v, ((0, 0), (0, 64 - v.shape[1])))


def _gdn_mixer(x2, b, l, w_in, conv_w, a_log, dt_bias, norm_w, w_out, x_res, ln_g, ln_b):
    t = b * l
    conv_ch = 4096
    main = _matmul(x2, w_in[:, :6144].astype(BF16))
    ab = _matmul(x2, w_in[:, 6144:], passes=3)
    qkv = _conv_silu(main.reshape(b, l, 6144), 0, conv_ch, conv_w.astype(F32),
                     jnp.zeros((1, conv_ch), F32), n_norm_blocks=4, n_scale_blocks=2,
                     head_dim=GDN_HEAD_DIM, scale=GDN_HEAD_DIM ** -0.5)
    per_slab = GDN_SLAB
    nslab = GDN_HEADS // per_slab
    heads = np.arange(GDN_HEADS)
    pf = _placement(heads, 32 + heads, nslab, per_slab)
    pb = _placement(16 + heads, 48 + heads, nslab, per_slab)
    col_f, col_b = _gate_prep(ab, _pad64(dt_bias), _pad64(a_log), pf, pb, aux_is_sigmoid=True)
    o_f = _gdn_scan(qkv, col_f.reshape(b, l, -1), _gdn_row_form(col_f, b, l), rev=False)
    o_b = _gdn_scan(qkv, col_b.reshape(b, l, -1), _gdn_row_form(col_b, b, l), rev=True)
    nw = jnp.tile(norm_w.astype(F32), GDN_HEADS).reshape(1, -1)
    return _out_ln(o_f.reshape(t, -1), o_b.reshape(t, -1), main, 2, nw, w_out.astype(BF16),
                   x_res, ln_g, ln_b, group=GDN_HEAD_DIM, gate_first=False)


def _mamba2_mixer(x2, b, l, w_in, conv_w, conv_b, a_log, dt_bias, d_skip, norm_w, w_out,
                  x_res, ln_g, ln_b):
    t = b * l
    d_inner = 2048
    conv_ch = 3072
    main = _matmul(x2, w_in[:, :d_inner + conv_ch].astype(BF16))
    dt = _matmul(x2, w_in[:, d_inner + conv_ch:], passes=3)
    xbc = _conv_silu(main.reshape(b, l, -1), d_inner, conv_ch, conv_w.astype(F32),
                     conv_b.reshape(1, -1).astype(F32))
    heads = np.arange(M2_GROUPS * M2_HEADS_PER_GROUP)
    pf = _placement(heads, heads, M2_GROUPS, M2_HEADS_PER_GROUP)
    pb = _placement(32 + heads, 32 + heads, M2_GROUPS, M2_HEADS_PER_GROUP)
    col_f, col_b = _gate_prep(dt, _pad64(dt_bias), _pad64(a_log), pf, pb, aux_is_sigmoid=False)
    lanes = np.arange(M2_HEADS_PER_GROUP * M2_HEAD_DIM)
    e8 = jnp.asarray(np.arange(M2_HEADS_PER_GROUP)[:, None] == lanes[None, :] // M2_HEAD_DIM, BF16)
    y_f = _ssd_scan(xbc, col_f.reshape(b, l, -1), _row_form(col_f, b, l), e8, rev=False)
    y_b = _ssd_scan(xbc, col_b.reshape(b, l, -1), _row_form(col_b, b, l), e8, rev=True)
    d_full = jnp.repeat(d_skip.astype(F32), M2_HEAD_DIM).reshape(1, -1)
    return _out_ln(y_f.reshape(t, -1), y_b.reshape(t, -1), main, 0,
                   norm_w.reshape(1, -1).astype(F32), w_out.astype(BF16), x_res, ln_g, ln_b,
                   group=M2_HEADS_PER_GROUP * M2_HEAD_DIM, gate_first=True,
                   skip=(xbc.reshape(t, -1), d_full))


def _hgrn2_mixer(x2, b, l, w_in, lb, norm_w, w_out, x_res, ln_g, ln_b):
    t = b * l
    proj = _matmul(x2, w_in.astype(BF16))
    lb = lb.reshape(1, -1).astype(F32)
    lbp = jnp.concatenate([1.0 - lb, jnp.log(lb), jnp.log1p(-lb), jnp.zeros((5, lb.shape[1]), F32)], axis=0)
    p3 = proj.reshape(b, l, -1)
    o_f = _hg_scan(p3, lbp, rev=False)
    o_b = _hg_scan(p3, lbp, rev=True)
    nw = jnp.tile(norm_w.astype(F32), HG_HEADS).reshape(1, -1)
    return _out_ln(o_f.reshape(t, -1), o_b.reshape(t, -1), proj, 4, nw, w_out.astype(BF16),
                   x_res, ln_g, ln_b, group=HG_DIM, gate_first=False)


def _gla_mixer(x2, b, l, w_in, w_gk, b_gk, norm_w, w_out, x_res, ln_g, ln_b):
    t = b * l
    main = _matmul(x2, w_in[:, :3072].astype(BF16))
    r = _matmul(x2, w_in[:, 3072:], passes=3)
    p3 = main.reshape(b, l, -1)
    r3 = r.reshape(b, l, -1)
    rank = w_gk.shape[1]
    zeros = jnp.zeros_like(w_gk[0])
    w_f = jnp.concatenate([w_gk[0], zeros], axis=0).astype(F32)
    w_b = jnp.concatenate([zeros, w_gk[1]], axis=0).astype(F32)
    del rank
    o_f = _gla_scan(p3, r3, w_f, b_gk[0].reshape(1, -1).astype(F32), rev=False)
    o_b = _gla_scan(p3, r3, w_b, b_gk[1].reshape(1, -1).astype(F32), rev=True)
    nw = jnp.tile(norm_w.astype(F32), GLA_HEADS).reshape(1, -1)
    return _out_ln(o_f.reshape(t, -1), o_b.reshape(t, -1), main, 2, nw, w_out.astype(BF16),
                   x_res, ln_g, ln_b, group=GLA_V, gate_first=False)


def _hier_moe_parts(x1, w_group, b_group, w_router, b_router, w_gate, w_up, w_down):
    d = x1.shape[1]
    wt = jnp.zeros((128, d), F32)
    wt = wt.at[0:MOE_GROUPS].set(w_group.T.astype(F32))
    wt = wt.at[8:8 + MOE_GROUPS * MOE_EPG].set(w_router.T.astype(F32))
    bias = jnp.zeros((128, 1), F32)
    bias = bias.at[0:MOE_GROUPS, 0].set(b_group.astype(F32))
    bias = bias.at[8:8 + MOE_GROUPS * MOE_EPG, 0].set(b_router.astype(F32))
    route_row, route_col, cnt = _router(x1, wt, bias)
    counts = cnt[:, :MOE_GROUPS, 0].astype(jnp.int32).reshape(-1)
    return _moe(x1, route_row, route_col, counts,
                w_gate.astype(BF16), w_up.astype(BF16), w_down.astype(BF16))


def kernel(x, p, gdn_w_in, gdn_conv_w, gdn_a_log, gdn_dt_bias, gdn_norm_w, gdn_w_out, m2_w_in, m2_conv_w, m2_conv_b, m2_a_log, m2_dt_bias, m2_d, m2_norm_w, m2_w_out, hg_w_in, hg_lb_logits, hg_norm_w, hg_w_out, gla_w_in, gla_w_gk, gla_b_gk, gla_norm_w, gla_w_out, ln_g, ln_b, moe_w_group, moe_b_group, moe_w_router, moe_b_router, moe_w_gate, moe_w_up, moe_w_down, pe_w_gate, pe_w_proj):
    b, l, d = x.shape
    t = b * l
    depth = p.shape[0]
    lb_all = jnp.cumsum(jax.nn.softmax(hg_lb_logits.astype(F32), axis=0), axis=0)
    lb_all = lb_all - lb_all[0]
    xc = x.reshape(t, d).astype(F32)
    for i in range(depth):
        m, j = i % 4, i // 4
        g1 = ln_g[i, 0].reshape(1, d).astype(F32)
        b1 = ln_b[i, 0].reshape(1, d).astype(F32)
        if m == 0:
            x1 = _gdn_mixer(xc, b, l, gdn_w_in[j], gdn_conv_w[j], gdn_a_log[j], gdn_dt_bias[j],
                            gdn_norm_w[j], gdn_w_out[j], xc, g1, b1)
        elif m == 1:
            x1 = _mamba2_mixer(xc, b, l, m2_w_in[j], m2_conv_w[j], m2_conv_b[j], m2_a_log[j],
                               m2_dt_bias[j], m2_d[j], m2_norm_w[j], m2_w_out[j], xc, g1, b1)
        elif m == 2:
            x1 = _hgrn2_mixer(xc, b, l, hg_w_in[j], lb_all[i], hg_norm_w[j], hg_w_out[j], xc, g1, b1)
        else:
            x1 = _gla_mixer(xc, b, l, gla_w_in[j], gla_w_gk[j], gla_b_gk[j], gla_norm_w[j],
                            gla_w_out[j], xc, g1, b1)
        parts = _hier_moe_parts(x1, moe_w_group[i], moe_b_group[i], moe_w_router[i], moe_b_router[i],
                                moe_w_gate[i], moe_w_up[i], moe_w_down[i])
        xc = _ln_pe(x1, parts, p[i].reshape(t, -1).astype(F32),
                    ln_g[i, 1].reshape(1, d).astype(F32), ln_b[i, 1].reshape(1, d).astype(F32),
                    pe_w_gate[i].astype(BF16), pe_w_proj[i].astype(BF16))
    return xc.reshape(b, l, d).astype(x.dtype)
```

```python
import functools
import math

import numpy as np
import jax
import jax.numpy as jnp
from jax import lax
from jax.experimental import pallas as pl
from jax.experimental.pallas import tpu as pltpu

F32 = jnp.float32
BF16 = jnp.bfloat16

CHUNK = 64
CONV_K = 5
LN_EPS = 1e-5
NORM_EPS = 1e-6
NEG_BIG = -1e30
DEPTH = 4
DEEPNORM_ALPHA = (2 * DEPTH) ** 0.25
GLA_GATE_TEMP = 16.0

VMEM_LIMIT_BYTES = 56 * 1024 * 1024

GDN_HEADS = 16
GDN_HEAD_DIM = 128
M2_GROUPS = 4
M2_HEADS_PER_GROUP = 8
M2_HEAD_DIM = 64
M2_STATE = 128
MOE_GROUPS = 4
MOE_EPG = 8
MOE_CAP = 288
MOE_TILE = 1024
ROUTE_ROWS = 16


def _cparams(sem):
    return pltpu.CompilerParams(dimension_semantics=sem, vmem_limit_bytes=VMEM_LIMIT_BYTES)


_NN = (((1,), (0,)), ((), ()))
_NT = (((1,), (1,)), ((), ()))
_TN = (((0,), (0,)), ((), ()))


def _dot(a, b, dims=_NN):
    return lax.dot_general(a.astype(BF16), b.astype(BF16), dims, preferred_element_type=F32)


def _split3(a):
    hi = a.astype(BF16)
    r1 = a - hi.astype(F32)
    mid = r1.astype(BF16)
    lo = (r1 - mid.astype(F32)).astype(BF16)
    return hi, mid, lo


def _dot_sel_l(sel, b, dims=_NN):
    s = sel.astype(BF16)
    out = None
    for part in _split3(b):
        t = lax.dot_general(s, part, dims, preferred_element_type=F32)
        out = t if out is None else out + t
    return out


def _dot_sel_r(a, sel, dims=_NN):
    s = sel.astype(BF16)
    out = None
    for part in _split3(a):
        t = lax.dot_general(part, s, dims, preferred_element_type=F32)
        out = t if out is None else out + t
    return out


def _dot_hi(a, b, dims=_NN):
    ah = a.astype(BF16)
    al = (a - ah.astype(F32)).astype(BF16)
    bh = b.astype(BF16)
    bl = (b - bh.astype(F32)).astype(BF16)
    out = lax.dot_general(ah, bh, dims, preferred_element_type=F32)
    out = out + lax.dot_general(ah, bl, dims, preferred_element_type=F32)
    out = out + lax.dot_general(al, bh, dims, preferred_element_type=F32)
    return out


def _silu(t):
    return t * jax.nn.sigmoid(t)


def _softplus(t):
    return jnp.maximum(t, 0.0) + jnp.log1p(jnp.exp(-jnp.abs(t)))


def _log_sigmoid(t):
    return jnp.minimum(t, 0.0) - jnp.log1p(jnp.exp(-jnp.abs(t)))


def _iota2(shape, axis):
    return lax.broadcasted_iota(jnp.int32, shape, axis)


def _mm_kernel(x_ref, w_ref, o_ref, *, passes):
    if passes == 1:
        o_ref[...] = jnp.dot(x_ref[...].astype(BF16), w_ref[...].astype(BF16),
                             preferred_element_type=F32)
    else:
        o_ref[...] = _dot_hi(x_ref[...], w_ref[...])


def _matmul(x, w, *, passes=1, tm=1024):
    t, k = x.shape
    n = w.shape[1]
    tm = min(tm, t)
    tn = n if n <= 1024 else (1024 if n % 1024 == 0 else 512)
    assert t % tm == 0 and n % tn == 0
    return pl.pallas_call(
        functools.partial(_mm_kernel, passes=passes),
        grid=(t // tm, n // tn),
        in_specs=[pl.BlockSpec((tm, k), lambda i, j: (i, 0)),
                  pl.BlockSpec((k, tn), lambda i, j: (0, j))],
        out_specs=pl.BlockSpec((tm, tn), lambda i, j: (i, j)),
        out_shape=jax.ShapeDtypeStruct((t, n), F32),
        compiler_params=_cparams(("parallel", "parallel")),
        name="proj_matmul",
    )(x, w)


def _conv_kernel(prev_ref, main_ref, next_ref, w_ref, b_ref, o_ref, buf_ref, *,
                 tl, n_norm_blocks, n_scale_blocks, head_dim, scale):
    i = pl.program_id(1)
    j = pl.program_id(2)
    last = pl.num_programs(1) - 1
    keep_prev = jnp.where(i == 0, 0.0, 1.0)
    keep_next = jnp.where(i == last, 0.0, 1.0)
    buf_ref[0:8, :] = prev_ref[0, 0] * keep_prev
    buf_ref[8:8 + tl, :] = main_ref[0]
    buf_ref[8 + tl:16 + tl, :] = next_ref[0, 0] * keep_next
    w = w_ref[...]
    acc = None
    for d in range(CONV_K):
        term = buf_ref[6 + d:6 + d + tl, :] * w[d:d + 1, :]
        acc = term if acc is None else acc + term
    y = _silu(acc + b_ref[...])
    if n_norm_blocks > 0:
        do_norm = j < n_norm_blocks
        fac = jnp.where(j < n_scale_blocks, scale, 1.0)
        pieces = []
        for h in range(y.shape[1] // head_dim):
            yh = y[:, h * head_dim:(h + 1) * head_dim]
            ss = jnp.sum(yh * yh, axis=1, keepdims=True)
            r = jnp.where(do_norm, lax.rsqrt(ss + NORM_EPS) * fac, 1.0)
            pieces.append(yh * r)
        y = jnp.concatenate(pieces, axis=1)
    o_ref[0] = y


def _conv_silu(proj, lane0, width, w, bias, *, n_norm_blocks=0, n_scale_blocks=0,
               head_dim=128, scale=1.0, tc=512):
    b, l, n = proj.shape
    tl = min(512, l)
    assert l % tl == 0 and width % tc == 0 and lane0 % tc == 0 and tl % 8 == 0
    off = lane0 // tc
    halo = proj.reshape(b, l // 8, 8, n)
    nl8 = l // 8
    step8 = tl // 8
    return pl.pallas_call(
        functools.partial(_conv_kernel, tl=tl, n_norm_blocks=n_norm_blocks,
                          n_scale_blocks=n_scale_blocks, head_dim=head_dim, scale=scale),
        grid=(b, l // tl, width // tc),
        in_specs=[
            pl.BlockSpec((1, 1, 8, tc), lambda bi, i, j: (bi, jnp.maximum(i * step8 - 1, 0), 0, off + j)),
            pl.BlockSpec((1, tl, tc), lambda bi, i, j: (bi, i, off + j)),
            pl.BlockSpec((1, 1, 8, tc), lambda bi, i, j: (bi, jnp.minimum((i + 1) * step8, nl8 - 1), 0, off + j)),
            pl.BlockSpec((CONV_K, tc), lambda bi, i, j: (0, j)),
            pl.BlockSpec((1, tc), lambda bi, i, j: (0, j)),
        ],
        out_specs=pl.BlockSpec((1, tl, tc), lambda bi, i, j: (bi, i, j)),
        out_shape=jax.ShapeDtypeStruct((b, l, width), F32),
        scratch_shapes=[pltpu.VMEM((tl + 16, tc), F32)],
        compiler_params=_cparams(("parallel", "parallel", "parallel")),
        name="conv_silu",
    )(halo, proj, halo, w, bias)


GATE_ROWS = 512


def _gate_prep_kernel(raw_ref, bias_ref, alog_ref, tri_l_ref, tri_u_ref, ones_ref,
                      pf_ref, pb_ref, of_ref, ob_ref, *, aux_is_sigmoid):
    raw = raw_ref[...]
    sp = _softplus(raw + bias_ref[...])
    g = -jnp.exp(alog_ref[...]) * sp
    aux = jax.nn.sigmoid(raw) if aux_is_sigmoid else sp
    gc_l = _dot_sel_l(tri_l_ref[...], g)
    gc_u = _dot_sel_l(tri_u_ref[...], g)
    gtot = _dot_sel_l(ones_ref[...], g)
    pf = pf_ref[...]
    pb = pb_ref[...]
    of_ref[...] = (_dot_sel_r(gc_l, pf[0]) + _dot_sel_r(aux, pf[1]) + _dot_sel_r(gtot, pf[2]))
    ob_ref[...] = (_dot_sel_r(gc_u, pb[0]) + _dot_sel_r(aux, pb[1]) + _dot_sel_r(gtot, pb[2]))


def _chunk_masks(rows):
    r = np.arange(rows)
    same = (r[:, None] // CHUNK) == (r[None, :] // CHUNK)
    tri_l = same & (r[None, :] <= r[:, None])
    tri_u = same & (r[None, :] >= r[:, None])
    return (jnp.asarray(tri_l, BF16), jnp.asarray(tri_u, BF16), jnp.asarray(same, BF16))


def _placement(src_gc, src_aux, n_slab, per_slab):
    p = np.zeros((3, 64, n_slab * 128), np.float32)
    for h in range(n_slab * per_slab):
        base = (h // per_slab) * 128 + h % per_slab
        p[0, src_gc[h], base] = 1.0
        p[1, src_aux[h], base + 32] = 1.0
        p[2, src_gc[h], base + 64] = 1.0
    return jnp.asarray(p, BF16)


def _gate_prep(raw, bias64, alog64, pf, pb, *, aux_is_sigmoid):
    t = raw.shape[0]
    rows = min(GATE_ROWS, t)
    w = pf.shape[2]
    tri_l, tri_u, ones = _chunk_masks(rows)
    full = lambda shape: pl.BlockSpec(shape, lambda i: (0,) * len(shape))
    return pl.pallas_call(
        functools.partial(_gate_prep_kernel, aux_is_sigmoid=aux_is_sigmoid),
        grid=(t // rows,),
        in_specs=[pl.BlockSpec((rows, 64), lambda i: (i, 0)),
                  full((1, 64)), full((1, 64)),
                  full((rows, rows)), full((rows, rows)), full((rows, rows)),
                  full((3, 64, w)), full((3, 64, w))],
        out_specs=[pl.BlockSpec((rows, w), lambda i: (i, 0)),
                   pl.BlockSpec((rows, w), lambda i: (i, 0))],
        out_shape=[jax.ShapeDtypeStruct((t, w), F32), jax.ShapeDtypeStruct((t, w), F32)],
        compiler_params=_cparams(("parallel",)),
        name="gate_prep",
    )(raw, bias64, alog64, tri_l, tri_u, ones, pf, pb)


def _row_form(col, b, l):
    w = col.shape[1]
    return jnp.swapaxes(col.reshape(b, l // CHUNK, CHUNK, w), 2, 3)


def _tri_masks(rev):
    ri = _iota2((CHUNK, CHUNK), 0)
    ci = _iota2((CHUNK, CHUNK), 1)
    if rev:
        return ri, ci, ci >= ri, ci > ri
    return ri, ci, ci <= ri, ci < ri


GDN_SLAB = 4
GDN_SLAB_ROWS = GDN_SLAB * CHUNK


GDN_LEVELS = int(math.log2(CHUNK))
GDN_MASK_EYE, GDN_MASK_INCL, GDN_MASK_STRICT, GDN_MASK_LEVEL0 = 0, 1, 2, 3


def _gdn_masks(rev):
    n = GDN_SLAB_ROWS
    r = np.arange(n)[:, None]
    c = np.arange(n)[None, :]
    same_head = (r // CHUNK) == (c // CHUNK)
    out = [r == c, same_head & ((c >= r) if rev else (c <= r)), same_head & ((c > r) if rev else (c < r))]
    s = 1
    while s < CHUNK:
        same = (r // (2 * s)) == (c // (2 * s))
        r_hi = (r % (2 * s)) >= s
        c_hi = (c % (2 * s)) >= s
        out.append(same & ~r_hi & c_hi if rev else same & r_hi & ~c_hi)
        s *= 2
    return jnp.asarray(np.stack(out), F32)


def _gdn_scan_kernel(qf_ref, kf_ref, vf_ref, gcf_ref, grf_ref, mf_ref,
                     qb_ref, kb_ref, vb_ref, gcb_ref, grb_ref, mb_ref, of_ref, ob_ref, s_ref):
    @pl.when(pl.program_id(1) == 0)
    def _():
        s_ref[...] = jnp.zeros_like(s_ref)

    _gdn_direction(qf_ref, kf_ref, vf_ref, gcf_ref, grf_ref, mf_ref, of_ref, s_ref, 0)
    _gdn_direction(qb_ref, kb_ref, vb_ref, gcb_ref, grb_ref, mb_ref, ob_ref, s_ref, GDN_HEADS // 2)


def _gdn_direction(q_ref, k_ref, v_ref, gcol_ref, grow_ref, m_ref, o_ref, s_ref, s_base):
    c = CHUNK
    d = GDN_HEAD_DIM
    first = _iota2((c, 2 * d), 1) < d
    pair_rows = _iota2((2 * c, 2 * d), 0) < c
    pair_lanes = _iota2((2 * c, 2 * d), 1) < d
    vn_mask = pair_rows == pair_lanes
    state_mask = (_iota2((2 * d, 2 * d), 0) < d) == (_iota2((2 * d, 2 * d), 1) < d)
    slabs = range(GDN_HEADS // GDN_SLAB)
    heads = range(GDN_SLAB)

    gcols, q_sts, k_sts, ps, abs_, ts, rhss = [], [], [], [], [], [], []
    for j in slabs:
        gcol = gcol_ref[0][:, 128 * j:128 * (j + 1)]
        grow = grow_ref[0, 0][j:j + 1, :]
        qk_head = [(GDN_SLAB * j + i) // 2 for i in heads]
        q_st = jnp.concatenate([q_ref[0][:, d * h:d * (h + 1)] for h in qk_head], axis=0)
        k_st = jnp.concatenate([k_ref[0][:, d * h:d * (h + 1)] for h in qk_head], axis=0)
        v_st = jnp.concatenate([v_ref[0][:, d * (GDN_SLAB * j + i):d * (GDN_SLAB * j + i + 1)]
                                for i in heads], axis=0)
        gc_c = jnp.concatenate([gcol[:, i:i + 1] for i in heads], axis=0)
        beta_c = jnp.concatenate([gcol[:, 32 + i:33 + i] for i in heads], axis=0)
        decay = jnp.exp(jnp.minimum(gc_c - grow, 0.0)) * m_ref[GDN_MASK_INCL]
        kb = k_st.astype(BF16)
        kk = lax.dot_general(kb, kb, _NT, preferred_element_type=F32)
        ps.append(lax.dot_general(q_st.astype(BF16), kb, _NT, preferred_element_type=F32) * decay)
        a = kk * decay * m_ref[GDN_MASK_STRICT] * beta_c
        abs_.append(a.astype(BF16))
        ts.append(m_ref[GDN_MASK_EYE] - a * m_ref[GDN_MASK_LEVEL0])
        rhss.append(jnp.concatenate([v_st * beta_c, k_st * (beta_c * jnp.exp(gc_c))], axis=1))
        gcols.append(gcol)
        q_sts.append(q_st)
        k_sts.append(k_st)

    for lvl in range(1, GDN_LEVELS):
        tbs = [t.astype(BF16) for t in ts]
        xs = [jnp.dot(tb, ab, preferred_element_type=F32).astype(BF16) for tb, ab in zip(tbs, abs_)]
        ts = [t - jnp.dot(x, tb, preferred_element_type=F32) * m_ref[GDN_MASK_LEVEL0 + lvl]
              for t, x, tb in zip(ts, xs, tbs)]
    uws = [_dot(t, rhs) for t, rhs in zip(ts, rhss)]

    for j in slabs:
        gcol, q_st, k_st, p, uw = gcols[j], q_sts[j], k_sts[j], ps[j], uws[j]
        for pr in range(GDN_SLAB // 2):
            ha, hb = 2 * pr, 2 * pr + 1
            ra = slice(c * ha, c * (ha + 1))
            rb = slice(c * hb, c * (hb + 1))
            u_pair = jnp.concatenate([uw[ra, :d], uw[rb, :d]], axis=1)
            w_pair = jnp.concatenate([uw[ra, d:], uw[rb, d:]], axis=1)
            q_pair = jnp.concatenate([q_st[ra], q_st[rb]], axis=1)
            k_pair = jnp.concatenate([k_st[ra], k_st[rb]], axis=1)
            gc_pair = jnp.where(first, gcol[:, ha:ha + 1], gcol[:, hb:hb + 1])
            gt_pair = jnp.where(first, gcol[:, 64 + ha:65 + ha], gcol[:, 64 + hb:65 + hb])
            idx = (GDN_SLAB // 2) * j + pr
            s = s_ref[s_base + idx]
            wq = _dot(jnp.concatenate([w_pair, q_pair * jnp.exp(gc_pair)], axis=0), s)
            v_new = u_pair - wq[:c]
            p_pair = p[ra, 2 * c * pr:2 * c * (pr + 1)] + p[rb, 2 * c * pr:2 * c * (pr + 1)]
            vn_bd = jnp.where(vn_mask, jnp.concatenate([v_new, v_new], axis=0), 0.0)
            o_ref[0, :, 2 * d * idx:2 * d * (idx + 1)] = wq[c:] + _dot(p_pair, vn_bd)
            k_dec = k_pair * jnp.exp(gt_pair - gc_pair)
            s_ref[s_base + idx] = (s * jnp.exp(gt_pair[0:1, :])
                                   + jnp.where(state_mask, _dot(k_dec, v_new, _TN), 0.0))


def _gdn_row_form(col, b, l):
    nslab = GDN_HEADS // GDN_SLAB
    g = col.reshape(b, l // CHUNK, CHUNK, nslab, 128)[..., :GDN_SLAB]
    g = jnp.transpose(g, (0, 1, 3, 4, 2)).reshape(b, l // CHUNK, nslab, GDN_SLAB_ROWS)
    return jnp.pad(g, ((0, 0), (0, 0), (0, 8 - nslab), (0, 0)))


def _gdn_scan(qkv, col_f, row_f, col_b, row_b):
    b, l, _ = qkv.shape
    nc = l // CHUNK
    kw = GDN_HEADS // 2 * GDN_HEAD_DIM
    vw = GDN_HEADS * GDN_HEAD_DIM
    masks_f = _gdn_masks(False)
    masks_b = _gdn_masks(True)

    def specs(cidx):
        return [
            pl.BlockSpec((1, CHUNK, kw), lambda bi, n: (bi, cidx(n), 0)),
            pl.BlockSpec((1, CHUNK, kw), lambda bi, n: (bi, cidx(n), 1)),
            pl.BlockSpec((1, CHUNK, vw), lambda bi, n: (bi, cidx(n), 1)),
            pl.BlockSpec((1, CHUNK, col_f.shape[2]), lambda bi, n: (bi, cidx(n), 0)),
            pl.BlockSpec((1, 1, 8, GDN_SLAB_ROWS), lambda bi, n: (bi, cidx(n), 0, 0)),
            pl.BlockSpec(masks_f.shape, lambda bi, n: (0, 0, 0)),
        ]

    fwd = lambda n: n
    bwd = lambda n: nc - 1 - n
    out = jax.ShapeDtypeStruct((b, l, vw), F32)
    return pl.pallas_call(
        _gdn_scan_kernel,
        grid=(b, nc),
        in_specs=specs(fwd) + specs(bwd),
        out_specs=[pl.BlockSpec((1, CHUNK, vw), lambda bi, n: (bi, n, 0)),
                   pl.BlockSpec((1, CHUNK, vw), lambda bi, n: (bi, nc - 1 - n, 0))],
        out_shape=[out, out],
        scratch_shapes=[pltpu.VMEM((GDN_HEADS, 2 * GDN_HEAD_DIM, 2 * GDN_HEAD_DIM), F32)],
        compiler_params=_cparams(("parallel", "arbitrary")),
        name="gdn_scan",
    )(qkv, qkv, qkv, col_f, row_f, masks_f, qkv, qkv, qkv, col_b, row_b, masks_b)


def _ssd_scan_kernel(c_ref, b_ref, x_ref, gcol_ref, grow_ref, e8_ref, y_ref, s_ref, *, rev):
    @pl.when(pl.program_id(1) == 0)
    def _():
        s_ref[...] = jnp.zeros_like(s_ref)

    _, _, incl, _ = _tri_masks(rev)
    e8 = e8_ref[...]
    r_heads = M2_HEADS_PER_GROUP
    gw = r_heads * M2_HEAD_DIM
    lane_head = _iota2((CHUNK, gw), 1) // M2_HEAD_DIM
    groups = range(M2_GROUPS)
    ccs = [c_ref[0][:, M2_STATE * g:M2_STATE * (g + 1)] for g in groups]
    bbs = [b_ref[0][:, M2_STATE * g:M2_STATE * (g + 1)] for g in groups]
    gcols = [gcol_ref[0][:, 128 * g:128 * (g + 1)] for g in groups]
    grows = [grow_ref[0, 0][128 * g:128 * (g + 1), :] for g in groups]
    gc_xs = [_dot_sel_r(gcols[g][:, 0:r_heads], e8) for g in groups]
    dt_xs = [_dot_sel_r(gcols[g][:, 32:32 + r_heads], e8) for g in groups]
    gt_xs = [_dot_sel_r(gcols[g][:, 64:64 + r_heads], e8) for g in groups]
    us = [x_ref[0][:, gw * g:gw * (g + 1)] * dt_xs[g] for g in groups]
    scores = [_dot(ccs[g], bbs[g], _NT) for g in groups]
    states = [s_ref[g] for g in groups]
    ys = [jnp.exp(gc_xs[g]) * _dot(ccs[g], states[g]) for g in groups]
    for r in range(r_heads):
        for g in groups:
            decay = jnp.exp(jnp.where(incl, gcols[g][:, r:r + 1] - grows[g][r:r + 1, :], NEG_BIG))
            ys[g] = ys[g] + _dot(scores[g] * decay, jnp.where(lane_head == r, us[g], 0.0))
    for g in groups:
        s_ref[g] = (states[g] * jnp.exp(gt_xs[g][0:1, :])
                    + _dot(bbs[g], us[g] * jnp.exp(gt_xs[g] - gc_xs[g]), _TN))
        y_ref[0, :, gw * g:gw * (g + 1)] = ys[g]


def _ssd_scan(xbc, col, row, e8, *, rev):
    b, l, _ = xbc.shape
    nc = l // CHUNK
    gw = M2_HEADS_PER_GROUP * M2_HEAD_DIM
    xw = M2_GROUPS * gw
    sw = M2_GROUPS * M2_STATE
    cidx = (lambda n: nc - 1 - n) if rev else (lambda n: n)
    return pl.pallas_call(
        functools.partial(_ssd_scan_kernel, rev=rev),
        grid=(b, nc),
        in_specs=[
            pl.BlockSpec((1, CHUNK, sw), lambda bi, n: (bi, cidx(n), xw // sw + 1)),
            pl.BlockSpec((1, CHUNK, sw), lambda bi, n: (bi, cidx(n), xw // sw)),
            pl.BlockSpec((1, CHUNK, xw), lambda bi, n: (bi, cidx(n), 0)),
            pl.BlockSpec((1, CHUNK, M2_GROUPS * 128), lambda bi, n: (bi, cidx(n), 0)),
            pl.BlockSpec((1, 1, M2_GROUPS * 128, CHUNK), lambda bi, n: (bi, cidx(n), 0, 0)),
            pl.BlockSpec((M2_HEADS_PER_GROUP, gw), lambda bi, n: (0, 0)),
        ],
        out_specs=pl.BlockSpec((1, CHUNK, xw), lambda bi, n: (bi, cidx(n), 0)),
        out_shape=jax.ShapeDtypeStruct((b, l, xw), F32),
        scratch_shapes=[pltpu.VMEM((M2_GROUPS, M2_STATE, gw), F32)],
        compiler_params=_cparams(("parallel", "arbitrary")),
        name="ssd_scan_bwd" if rev else "ssd_scan_fwd",
    )(xbc, xbc, xbc, col, row, e8)


def _gla_chunk(qs_, ks_, vs_, gs_, sts_, rev):
    c = CHUNK
    nh = len(qs_)
    hr = range(nh)
    ri, ci, incl, _ = _tri_masks(rev)
    tri = jnp.where(incl, 1.0, 0.0)
    gsums = [_dot_sel_l(tri, gs_[h]) for h in hr]
    gtots = [gs[0:1] if rev else gs[c - 1:c] for gs in gsums]
    os_ = [_dot(qs_[h] * jnp.exp(gsums[h]), sts_[h], _NT) for h in hr]
    st_new = [sts_[h] * jnp.exp(gtots[h]) + _dot(vs_[h], ks_[h] * jnp.exp(gtots[h] - gsums[h]), _TN)
              for h in hr]

    row = _iota2((c, 1), 0)
    attns = [jnp.zeros((c, c), F32) for _ in hr]
    s = c // 2
    while s >= 8:
        upper_half = (row & (2 * s - 1)) >= s
        q_rows = (~upper_half) if rev else upper_half
        shift = int(math.log2(2 * s))
        same_pair = (ri >> shift) == (ci >> shift)
        prods = []
        for h in hr:
            gsum = gsums[h]
            pieces = []
            for p in range(c // (2 * s)):
                r = p * 2 * s + (s if rev else s - 1)
                pieces.append(jnp.broadcast_to(gsum[r:r + 1], (2 * s, gsum.shape[1])))
            ref = pieces[0] if len(pieces) == 1 else jnp.concatenate(pieces, axis=0)
            e = jnp.exp(-jnp.abs(gsum - ref))
            q_s = jnp.where(q_rows, qs_[h] * e, 0.0)
            k_s = jnp.where(q_rows, 0.0, ks_[h] * e)
            prods.append(_dot(q_s, k_s, _NT))
        attns = [attns[h] + jnp.where(same_pair, prods[h], 0.0) for h in hr]
        s //= 2

    lane = _iota2((8, c), 1)
    sub = _iota2((8, 1), 0)
    blocks = [[] for _ in hr]
    for blk in range(c // 8):
        rows = slice(8 * blk, 8 * blk + 8)
        accs = [jnp.zeros((8, c), F32) for _ in hr]
        for j in range(8):
            ok = (lane == 8 * blk + j) & ((sub <= j) if rev else (sub >= j))
            for h in hr:
                gb = gsums[h][rows]
                e = jnp.exp(jnp.minimum(gb - gb[j:j + 1], 0.0))
                red = jnp.sum(qs_[h][rows] * ks_[h][rows][j:j + 1] * e, axis=1, keepdims=True)
                accs[h] = jnp.where(ok, red, accs[h])
        for h in hr:
            blocks[h].append(accs[h])
    os_ = [os_[h] + _dot(attns[h] + jnp.concatenate(blocks[h], axis=0), vs_[h]) for h in hr]
    return os_, st_new


HG_HB = 8
HG_HEADS = 8
HG_DIM = 128


def _hg_scan_kernel(q_ref, f_ref, v_ref, lb_ref, o_ref, st_ref, *, hb, rev):
    @pl.when(pl.program_id(2) == 0)
    def _():
        st_ref[...] = jnp.zeros_like(st_ref)

    lbp = lb_ref[...]
    d = HG_DIM
    qs, ks, vs, gs = [], [], [], []
    for i in range(hb):
        sl = slice(i * d, (i + 1) * d)
        f = f_ref[0][:, sl]
        a = lbp[1:2, sl]
        bterm = lbp[2:3, sl] + _log_sigmoid(f)
        qs.append(_silu(q_ref[0][:, sl]))
        ks.append(lbp[0:1, sl] * jax.nn.sigmoid(-f))
        vs.append(v_ref[0][:, sl])
        gs.append(jnp.maximum(a, bterm) + jnp.log1p(jnp.exp(-jnp.abs(a - bterm))))
    outs, st_new = _gla_chunk(qs, ks, vs, gs, [st_ref[i] for i in range(hb)], rev)
    for i in range(hb):
        st_ref[i] = st_new[i]
    o_ref[0] = jnp.concatenate(outs, axis=1)


def _hg_scan(proj, lbp, *, rev):
    b, l, _ = proj.shape
    nc = l // CHUNK
    hb = HG_HB
    nsl = HG_HEADS // hb
    f_off = (2 if rev else 1) * nsl
    cidx = (lambda n: nc - 1 - n) if rev else (lambda n: n)
    blk = (1, CHUNK, hb * HG_DIM)
    return pl.pallas_call(
        functools.partial(_hg_scan_kernel, hb=hb, rev=rev),
        grid=(b, nsl, nc),
        in_specs=[
            pl.BlockSpec(blk, lambda bi, j, n: (bi, cidx(n), j)),
            pl.BlockSpec(blk, lambda bi, j, n: (bi, cidx(n), f_off + j)),
            pl.BlockSpec(blk, lambda bi, j, n: (bi, cidx(n), 3 * nsl + j)),
            pl.BlockSpec((8, hb * HG_DIM), lambda bi, j, n: (0, j)),
        ],
        out_specs=pl.BlockSpec(blk, lambda bi, j, n: (bi, cidx(n), j)),
        out_shape=jax.ShapeDtypeStruct((b, l, HG_HEADS * HG_DIM), F32),
        scratch_shapes=[pltpu.VMEM((hb, HG_DIM, HG_DIM), F32)],
        compiler_params=_cparams(("parallel", "parallel", "arbitrary")),
        name="hgrn2_scan_bwd" if rev else "hgrn2_scan_fwd",
    )(proj, proj, proj, lbp)


GLA_HB = 4
GLA_HEADS = 4
GLA_K = 128
GLA_V = 256


def _gla_scan_kernel(q_ref, k_ref, v_ref, r_ref, wgk_ref, bgk_ref, o_ref, st_ref, *, hb, rev):
    @pl.when(pl.program_id(2) == 0)
    def _():
        st_ref[...] = jnp.zeros_like(st_ref)

    pre = _dot_hi(r_ref[0], wgk_ref[...]) + bgk_ref[...]
    gk = _log_sigmoid(pre) * (1.0 / GLA_GATE_TEMP)
    sls = [slice(i * GLA_K, (i + 1) * GLA_K) for i in range(hb)]
    outs, st_new = _gla_chunk([q_ref[0][:, sl] * (GLA_K ** -0.5) for sl in sls],
                              [k_ref[0][:, sl] for sl in sls],
                              [v_ref[0][:, i * GLA_V:(i + 1) * GLA_V] for i in range(hb)],
                              [gk[:, sl] for sl in sls],
                              [st_ref[i] for i in range(hb)], rev)
    for i in range(hb):
        st_ref[i] = st_new[i]
    o_ref[0] = jnp.concatenate(outs, axis=1)


def _gla_scan(proj, r, wgk, bgk, *, rev):
    b, l, _ = proj.shape
    nc = l // CHUNK
    hb = GLA_HB
    nsl = GLA_HEADS // hb
    cidx = (lambda n: nc - 1 - n) if rev else (lambda n: n)
    return pl.pallas_call(
        functools.partial(_gla_scan_kernel, hb=hb, rev=rev),
        grid=(b, nsl, nc),
        in_specs=[
            pl.BlockSpec((1, CHUNK, hb * GLA_K), lambda bi, j, n: (bi, cidx(n), j)),
            pl.BlockSpec((1, CHUNK, hb * GLA_K), lambda bi, j, n: (bi, cidx(n), nsl + j)),
            pl.BlockSpec((1, CHUNK, hb * GLA_V), lambda bi, j, n: (bi, cidx(n), nsl + j)),
            pl.BlockSpec((1, CHUNK, r.shape[2]), lambda bi, j, n: (bi, cidx(n), 0)),
            pl.BlockSpec((wgk.shape[0], hb * GLA_K), lambda bi, j, n: (0, j)),
            pl.BlockSpec((1, hb * GLA_K), lambda bi, j, n: (0, j)),
        ],
        out_specs=pl.BlockSpec((1, CHUNK, hb * GLA_V), lambda bi, j, n: (bi, cidx(n), j)),
        out_shape=jax.ShapeDtypeStruct((b, l, GLA_HEADS * GLA_V), F32),
        scratch_shapes=[pltpu.VMEM((hb, GLA_V, GLA_K), F32)],
        compiler_params=_cparams(("parallel", "parallel", "arbitrary")),
        name="gla_scan_bwd" if rev else "gla_scan_fwd",
    )(proj, proj, proj, r, wgk, bgk)


def _layernorm_rows(t, g, b):
    mu = jnp.mean(t, axis=1, keepdims=True)
    tc = t - mu
    var = jnp.mean(tc * tc, axis=1, keepdims=True)
    return tc * lax.rsqrt(var + LN_EPS) * g + b


def _out_ln_kernel(*refs, group, gate_first, has_skip):
    if has_skip:
        of_ref, ob_ref, z_ref, xs_ref, d_ref, nw_ref, w_ref, x_ref, g_ref, b_ref, o_ref = refs
        y = of_ref[...] + ob_ref[...] + d_ref[...] * xs_ref[...]
    else:
        of_ref, ob_ref, z_ref, nw_ref, w_ref, x_ref, g_ref, b_ref, o_ref = refs
        y = of_ref[...] + ob_ref[...]
    gate = _silu(z_ref[...])
    if gate_first:
        y = y * gate
    pieces = []
    for h in range(y.shape[1] // group):
        yh = y[:, h * group:(h + 1) * group]
        ms = jnp.mean(yh * yh, axis=1, keepdims=True)
        pieces.append(yh * lax.rsqrt(ms + NORM_EPS))
    y = jnp.concatenate(pieces, axis=1) * nw_ref[...]
    if not gate_first:
        y = y * gate
    h_out = _dot(y, w_ref[...])
    o_ref[...] = _layernorm_rows(DEEPNORM_ALPHA * x_ref[...] + h_out, g_ref[...], b_ref[...])


def _out_ln(o_f, o_b, z_arr, z_block, nw_full, w_out, x_res, ln_g, ln_b, *, group, gate_first,
            skip=None, tm=256):
    t, din = o_f.shape
    d = w_out.shape[1]
    tm = min(tm, t)
    row = lambda width, blk=0: pl.BlockSpec((tm, width), lambda i: (i, blk))
    full = lambda shape: pl.BlockSpec(shape, lambda i: (0,) * len(shape))
    ins = [o_f, o_b, z_arr]
    specs = [row(din), row(din), row(din, z_block)]
    if skip is not None:
        xs_arr, d_full = skip
        ins += [xs_arr, d_full]
        specs += [row(din), full((1, din))]
    ins += [nw_full, w_out, x_res, ln_g, ln_b]
    specs += [full((1, din)), full((din, d)), row(d), full((1, d)), full((1, d))]
    return pl.pallas_call(
        functools.partial(_out_ln_kernel, group=group, gate_first=gate_first, has_skip=skip is not None),
        grid=(t // tm,),
        in_specs=specs,
        out_specs=row(d),
        out_shape=jax.ShapeDtypeStruct((t, d), F32),
        compiler_params=_cparams(("parallel",)),
        name="gated_norm_out_ln",
    )(*ins)


def _router_kernel(x_ref, wt_ref, bias_ref, tri_ref, row_ref, col_ref, cnt_ref):
    x = x_ref[...]
    tm = x.shape[0]
    logits = _dot_hi(wt_ref[...], x, _NT) + bias_ref[...]
    sub8 = _iota2((8, tm), 0).astype(F32)
    gl = jnp.where(sub8 < MOE_GROUPS, logits[0:8], NEG_BIG)
    gmax = jnp.max(gl, axis=0, keepdims=True)
    gidx = jnp.min(jnp.where(gl == gmax, sub8, 8.0), axis=0, keepdims=True)
    gprob = 1.0 / jnp.sum(jnp.exp(gl - gmax), axis=0, keepdims=True)
    el = logits[8:16]
    for g in range(1, MOE_GROUPS):
        el = jnp.where(gidx == g, logits[8 + 8 * g:16 + 8 * g], el)
    m1 = jnp.max(el, axis=0, keepdims=True)
    i1 = jnp.min(jnp.where(el == m1, sub8, 8.0), axis=0, keepdims=True)
    el2 = jnp.where(sub8 == i1, NEG_BIG, el)
    m2 = jnp.max(el2, axis=0, keepdims=True)
    i2 = jnp.min(jnp.where(el2 == m2, sub8, 8.0), axis=0, keepdims=True)
    e2 = jnp.exp(m2 - m1)
    w1 = gprob / (1.0 + e2)
    w2 = gprob * e2 / (1.0 + e2)
    within = jnp.where(sub8 == i1, w1, 0.0) + jnp.where(sub8 == i2, w2, 0.0)
    onehot = jnp.where(sub8 == gidx, 1.0, 0.0)
    before = _dot(onehot, tri_ref[...])
    pos = jnp.sum(onehot * before, axis=0, keepdims=True)
    rec = jnp.where(sub8 == 0, gidx, jnp.where(sub8 == 1, pos, 0.0))
    rows = jnp.concatenate([within, rec], axis=0)
    row_ref[...] = rows
    padded = jnp.concatenate([rows, jnp.zeros((128 - ROUTE_ROWS, tm), F32)], axis=0)
    col_ref[...] = padded.T
    cnt_ref[0] = jnp.broadcast_to(jnp.sum(onehot, axis=1, keepdims=True), (8, 128))


def _router(x, wt, bias_col):
    t, d = x.shape
    tm = min(MOE_TILE, t)
    nt = t // tm
    r = np.arange(tm)
    tri = jnp.asarray(r[:, None] < r[None, :], BF16)
    full = lambda shape: pl.BlockSpec(shape, lambda i: (0,) * len(shape))
    return pl.pallas_call(
        _router_kernel,
        grid=(nt,),
        in_specs=[pl.BlockSpec((tm, d), lambda i: (i, 0)), full((128, d)), full((128, 1)), full((tm, tm))],
        out_specs=[pl.BlockSpec((ROUTE_ROWS, tm), lambda i: (0, i)),
                   pl.BlockSpec((tm, 128), lambda i: (i, 0)),
                   pl.BlockSpec((1, 8, 128), lambda i: (i, 0, 0))],
        out_shape=[jax.ShapeDtypeStruct((ROUTE_ROWS, t), F32),
                   jax.ShapeDtypeStruct((t, 128), F32),
                   jax.ShapeDtypeStruct((nt, 8, 128), F32)],
        compiler_params=_cparams(("parallel",)),
        name="moe_router",
    )(x, wt, bias_col, tri)


def _moe_kernel(cnt_ref, x_ref, row_ref, col_ref, wg_ref, wu_ref, wd_ref, o_ref, *, cap):
    g = pl.program_id(0)
    ti = pl.program_id(1)
    n = cnt_ref[ti * MOE_GROUPS + g]
    nb = (n + cap - 1) // cap
    tm = x_ref.shape[0]
    gf = g.astype(F32)
    rows = row_ref[...]
    cols = col_ref[...]
    in_g_row = rows[8:9, :] == gf
    pos_row = rows[9:10, :]
    in_g_col = cols[:, 8:9] == gf
    pos_col = cols[:, 9:10]
    xb = x_ref[...].astype(BF16)
    o_ref[0] = jnp.zeros((tm, o_ref.shape[2]), F32)

    def body(bi, carry):
        base = (bi * cap).astype(F32)
        slot_r = _iota2((cap, tm), 0).astype(F32) + base
        sel = jnp.where(in_g_row & (slot_r == pos_row), 1.0, 0.0).astype(BF16)
        xg = jnp.dot(sel, xb, preferred_element_type=F32).astype(BF16)
        cw = _dot_sel_l(sel, cols)
        y = jnp.zeros((cap, o_ref.shape[2]), F32)
        for e in range(MOE_EPG):
            hg = jnp.dot(xg, wg_ref[0, e], preferred_element_type=F32)
            hu = jnp.dot(xg, wu_ref[0, e], preferred_element_type=F32)
            h = _silu(hg) * hu * cw[:, e:e + 1]
            y = y + _dot(h, wd_ref[0, e])
        slot_c = _iota2((tm, cap), 1).astype(F32) + base
        sel_t = jnp.where(in_g_col & (slot_c == pos_col), 1.0, 0.0).astype(BF16)
        o_ref[0] += jnp.dot(sel_t, y.astype(BF16), preferred_element_type=F32)
        return carry

    lax.fori_loop(0, nb, body, 0)


def _moe(x, route_row, route_col, counts, wg, wu, wd):
    t, d = x.shape
    tm = min(MOE_TILE, t)
    cap = min(MOE_CAP, tm)
    nt = t // tm
    hid = wg.shape[3]
    grid_spec = pltpu.PrefetchScalarGridSpec(
        num_scalar_prefetch=1,
        grid=(MOE_GROUPS, nt),
        in_specs=[
            pl.BlockSpec((tm, d), lambda g, i, c: (i, 0)),
            pl.BlockSpec((ROUTE_ROWS, tm), lambda g, i, c: (0, i)),
            pl.BlockSpec((tm, 128), lambda g, i, c: (i, 0)),
            pl.BlockSpec((1, MOE_EPG, d, hid), lambda g, i, c: (g, 0, 0, 0)),
            pl.BlockSpec((1, MOE_EPG, d, hid), lambda g, i, c: (g, 0, 0, 0)),
            pl.BlockSpec((1, MOE_EPG, hid, d), lambda g, i, c: (g, 0, 0, 0)),
        ],
        out_specs=pl.BlockSpec((1, tm, d), lambda g, i, c: (g, i, 0)),
    )
    return pl.pallas_call(
        functools.partial(_moe_kernel, cap=cap),
        grid_spec=grid_spec,
        out_shape=jax.ShapeDtypeStruct((MOE_GROUPS, t, d), F32),
        compiler_params=_cparams(("arbitrary", "arbitrary")),
        name="moe_experts",
    )(counts, x, route_row, route_col, wg, wu, wd)


def _ln_pe_kernel(x_ref, m_ref, p_ref, g_ref, b_ref, wg_ref, wp_ref, o_ref):
    h = m_ref[0]
    for g in range(1, MOE_GROUPS):
        h = h + m_ref[g]
    y = _layernorm_rows(DEEPNORM_ALPHA * x_ref[...] + h, g_ref[...], b_ref[...])
    gate = jax.nn.sigmoid(_dot(y, wg_ref[...]))
    o_ref[...] = y + gate * _dot(p_ref[...], wp_ref[...])


def _ln_pe(x, moe_parts, p, ln_g, ln_b, w_gate, w_proj, *, tm=512):
    t, d = x.shape
    pd = p.shape[1]
    tm = min(tm, t)
    full = lambda shape: pl.BlockSpec(shape, lambda i: (0,) * len(shape))
    return pl.pallas_call(
        _ln_pe_kernel,
        grid=(t // tm,),
        in_specs=[pl.BlockSpec((tm, d), lambda i: (i, 0)),
                  pl.BlockSpec((MOE_GROUPS, tm, d), lambda i: (0, i, 0)),
                  pl.BlockSpec((tm, pd), lambda i: (i, 0)),
                  full((1, d)), full((1, d)), full((d, d)), full((pd, d))],
        out_specs=pl.BlockSpec((tm, d), lambda i: (i, 0)),
        out_shape=jax.ShapeDtypeStruct((t, d), F32),
        compiler_params=_cparams(("parallel",)),
        name="ln_input_embedding",
    )(x, moe_parts, p, ln_g, ln_b, w_gate, w_proj)


def _pad64(v):
    v = v.reshape(1, -1).astype(F32)
    return jnp.pad(v, ((0, 0), (0, 64 - v.shape[1])))


def _gdn_mixer(x2, b, l, w_in, conv_w, a_log, dt_bias, norm_w, w_out, x_res, ln_g, ln_b):
    t = b * l
    conv_ch = 4096
    main = _matmul(x2, w_in[:, :6144].astype(BF16))
    ab = _matmul(x2, w_in[:, 6144:], passes=3)
    qkv = _conv_silu(main.reshape(b, l, 6144), 0, conv_ch, conv_w.astype(F32),
                     jnp.zeros((1, conv_ch), F32), n_norm_blocks=4, n_scale_blocks=2,
                     head_dim=GDN_HEAD_DIM, scale=GDN_HEAD_DIM ** -0.5)
    per_slab = GDN_SLAB
    nslab = GDN_HEADS // per_slab
    heads = np.arange(GDN_HEADS)
    pf = _placement(heads, 32 + heads, nslab, per_slab)
    pb = _placement(16 + heads, 48 + heads, nslab, per_slab)
    col_f, col_b = _gate_prep(ab, _pad64(dt_bias), _pad64(a_log), pf, pb, aux_is_sigmoid=True)
    o_f, o_b = _gdn_scan(qkv, col_f.reshape(b, l, -1), _gdn_row_form(col_f, b, l),
                         col_b.reshape(b, l, -1), _gdn_row_form(col_b, b, l))
    nw = jnp.tile(norm_w.astype(F32), GDN_HEADS).reshape(1, -1)
    return _out_ln(o_f.reshape(t, -1), o_b.reshape(t, -1), main, 2, nw, w_out.astype(BF16),
                   x_res, ln_g, ln_b, group=GDN_HEAD_DIM, gate_first=False)


def _mamba2_mixer(x2, b, l, w_in, conv_w, conv_b, a_log, dt_bias, d_skip, norm_w, w_out,
                  x_res, ln_g, ln_b):
    t = b * l
    d_inner = 2048
    conv_ch = 3072
    main = _matmul(x2, w_in[:, :d_inner + conv_ch].astype(BF16))
    dt = _matmul(x2, w_in[:, d_inner + conv_ch:], passes=3)
    xbc = _conv_silu(main.reshape(b, l, -1), d_inner, conv_ch, conv_w.astype(F32),
                     conv_b.reshape(1, -1).astype(F32))
    heads = np.arange(M2_GROUPS * M2_HEADS_PER_GROUP)
    pf = _placement(heads, heads, M2_GROUPS, M2_HEADS_PER_GROUP)
    pb = _placement(32 + heads, 32 + heads, M2_GROUPS, M2_HEADS_PER_GROUP)
    col_f, col_b = _gate_prep(dt, _pad64(dt_bias), _pad64(a_log), pf, pb, aux_is_sigmoid=False)
    lanes = np.arange(M2_HEADS_PER_GROUP * M2_HEAD_DIM)
    e8 = jnp.asarray(np.arange(M2_HEADS_PER_GROUP)[:, None] == lanes[None, :] // M2_HEAD_DIM, BF16)
    y_f = _ssd_scan(xbc, col_f.reshape(b, l, -1), _row_form(col_f, b, l), e8, rev=False)
    y_b = _ssd_scan(xbc, col_b.reshape(b, l, -1), _row_form(col_b, b, l), e8, rev=True)
    d_full = jnp.repeat(d_skip.astype(F32), M2_HEAD_DIM).reshape(1, -1)
    return _out_ln(y_f.reshape(t, -1), y_b.reshape(t, -1), main, 0,
                   norm_w.reshape(1, -1).astype(F32), w_out.astype(BF16), x_res, ln_g, ln_b,
                   group=M2_HEADS_PER_GROUP * M2_HEAD_DIM, gate_first=True,
                   skip=(xbc.reshape(t, -1), d_full))


def _hgrn2_mixer(x2, b, l, w_in, lb, norm_w, w_out, x_res, ln_g, ln_b):
    t = b * l
    proj = _matmul(x2, w_in.astype(BF16))
    lb = lb.reshape(1, -1).astype(F32)
    lbp = jnp.concatenate([1.0 - lb, jnp.log(lb), jnp.log1p(-lb), jnp.zeros((5, lb.shape[1]), F32)], axis=0)
    p3 = proj.reshape(b, l, -1)
    o_f = _hg_scan(p3, lbp, rev=False)
    o_b = _hg_scan(p3, lbp, rev=True)
    nw = jnp.tile(norm_w.astype(F32), HG_HEADS).reshape(1, -1)
    return _out_ln(o_f.reshape(t, -1), o_b.reshape(t, -1), proj, 4, nw, w_out.astype(BF16),
                   x_res, ln_g, ln_b, group=HG_DIM, gate_first=False)


def _gla_mixer(x2, b, l, w_in, w_gk, b_gk, norm_w, w_out, x_res, ln_g, ln_b):
    t = b * l
    main = _matmul(x2, w_in[:, :3072].astype(BF16))
    r = _matmul(x2, w_in[:, 3072:], passes=3)
    p3 = main.reshape(b, l, -1)
    r3 = r.reshape(b, l, -1)
    rank = w_gk.shape[1]
    zeros = jnp.zeros_like(w_gk[0])
    w_f = jnp.concatenate([w_gk[0], zeros], axis=0).astype(F32)
    w_b = jnp.concatenate([zeros, w_gk[1]], axis=0).astype(F32)
    del rank
    o_f = _gla_scan(p3, r3, w_f, b_gk[0].reshape(1, -1).astype(F32), rev=False)
    o_b = _gla_scan(p3, r3, w_b, b_gk[1].reshape(1, -1).astype(F32), rev=True)
    nw = jnp.tile(norm_w.astype(F32), GLA_HEADS).reshape(1, -1)
    return _out_ln(o_f.reshape(t, -1), o_b.reshape(t, -1), main, 2, nw, w_out.astype(BF16),
                   x_res, ln_g, ln_b, group=GLA_V, gate_first=False)


def _hier_moe_parts(x1, w_group, b_group, w_router, b_router, w_gate, w_up, w_down):
    d = x1.shape[1]
    wt = jnp.zeros((128, d), F32)
    wt = wt.at[0:MOE_GROUPS].set(w_group.T.astype(F32))
    wt = wt.at[8:8 + MOE_GROUPS * MOE_EPG].set(w_router.T.astype(F32))
    bias = jnp.zeros((128, 1), F32)
    bias = bias.at[0:MOE_GROUPS, 0].set(b_group.astype(F32))
    bias = bias.at[8:8 + MOE_GROUPS * MOE_EPG, 0].set(b_router.astype(F32))
    route_row, route_col, cnt = _router(x1, wt, bias)
    counts = cnt[:, :MOE_GROUPS, 0].astype(jnp.int32).reshape(-1)
    return _moe(x1, route_row, route_col, counts,
                w_gate.astype(BF16), w_up.astype(BF16), w_down.astype(BF16))


def kernel(x, p, gdn_w_in, gdn_conv_w, gdn_a_log, gdn_dt_bias, gdn_norm_w, gdn_w_out, m2_w_in, m2_conv_w, m2_conv_b, m2_a_log, m2_dt_bias, m2_d, m2_norm_w, m2_w_out, hg_w_in, hg_lb_logits, hg_norm_w, hg_w_out, gla_w_in, gla_w_gk, gla_b_gk, gla_norm_w, gla_w_out, ln_g, ln_b, moe_w_group, moe_b_group, moe_w_router, moe_b_router, moe_w_gate, moe_w_up, moe_w_down, pe_w_gate, pe_w_proj):
    b, l, d = x.shape
    t = b * l
    depth = p.shape[0]
    lb_all = jnp.cumsum(jax.nn.softmax(hg_lb_logits.astype(F32), axis=0), axis=0)
    lb_all = lb_all - lb_all[0]
    xc = x.reshape(t, d).astype(F32)
    for i in range(depth):
        m, j = i % 4, i // 4
        g1 = ln_g[i, 0].reshape(1, d).astype(F32)
        b1 = ln_b[i, 0].reshape(1, d).astype(F32)
        if m == 0:
            x1 = _gdn_mixer(xc, b, l, gdn_w_in[j], gdn_conv_w[j], gdn_a_log[j], gdn_dt_bias[j],
                            gdn_norm_w[j], gdn_w_out[j], xc, g1, b1)
        elif m == 1:
            x1 = _mamba2_mixer(xc, b, l, m2_w_in[j], m2_conv_w[j], m2_conv_b[j], m2_a_log[j],
                               m2_dt_bias[j], m2_d[j], m2_norm_w[j], m2_w_out[j], xc, g1, b1)
        elif m == 2:
            x1 = _hgrn2_mixer(xc, b, l, hg_w_in[j], lb_all[i], hg_norm_w[j], hg_w_out[j], xc, g1, b1)
        else:
            x1 = _gla_mixer(xc, b, l, gla_w_in[j], gla_w_gk[j], gla_b_gk[j], gla_norm_w[j],
                            gla_w_out[j], xc, g1, b1)
        parts = _hier_moe_parts(x1, moe_w_group[i], moe_b_group[i], moe_w_router[i], moe_b_router[i],
                                moe_w_gate[i], moe_w_up[i], moe_w_down[i])
        xc = _ln_pe(x1, parts, p[i].reshape(t, -1).astype(F32),
                    ln_g[i, 1].reshape(1, d).astype(F32), ln_b[i, 1].reshape(1, d).astype(F32),
                    pe_w_gate[i].astype(BF16), pe_w_proj[i].astype(BF16))
    return xc.reshape(b, l, d).astype(x.dtype)
```

```python
import functools
import math

import numpy as np
import jax
import jax.numpy as jnp
from jax import lax
from jax.experimental import pallas as pl
from jax.experimental.pallas import tpu as pltpu

F32 = jnp.float32
BF16 = jnp.bfloat16

CHUNK = 64
CONV_K = 5
LN_EPS = 1e-5
NORM_EPS = 1e-6
NEG_BIG = -1e30
DEPTH = 4
DEEPNORM_ALPHA = (2 * DEPTH) ** 0.25
GLA_GATE_TEMP = 16.0

VMEM_LIMIT_BYTES = 56 * 1024 * 1024

GDN_HEADS = 16
GDN_HEAD_DIM = 128
M2_GROUPS = 4
M2_HEADS_PER_GROUP = 8
M2_HEAD_DIM = 64
M2_STATE = 128
MOE_GROUPS = 4
MOE_EPG = 8
MOE_CAP = 320
MOE_TILE = 1024
ROUTE_ROWS = 16


def _cparams(sem):
    return pltpu.CompilerParams(dimension_semantics=sem, vmem_limit_bytes=VMEM_LIMIT_BYTES)


_NN = (((1,), (0,)), ((), ()))
_NT = (((1,), (1,)), ((), ()))
_TN = (((0,), (0,)), ((), ()))


def _dot(a, b, dims=_NN):
    return lax.dot_general(a.astype(BF16), b.astype(BF16), dims, preferred_element_type=F32)


def _split3(a):
    hi = a.astype(BF16)
    r1 = a - hi.astype(F32)
    mid = r1.astype(BF16)
    lo = (r1 - mid.astype(F32)).astype(BF16)
    return hi, mid, lo


def _dot_sel_l(sel, b, dims=_NN):
    s = sel.astype(BF16)
    out = None
    for part in _split3(b):
        t = lax.dot_general(s, part, dims, preferred_element_type=F32)
        out = t if out is None else out + t
    return out


def _dot_sel_r(a, sel, dims=_NN):
    s = sel.astype(BF16)
    out = None
    for part in _split3(a):
        t = lax.dot_general(part, s, dims, preferred_element_type=F32)
        out = t if out is None else out + t
    return out


def _dot_hi(a, b, dims=_NN):
    ah = a.astype(BF16)
    al = (a - ah.astype(F32)).astype(BF16)
    bh = b.astype(BF16)
    bl = (b - bh.astype(F32)).astype(BF16)
    out = lax.dot_general(ah, bh, dims, preferred_element_type=F32)
    out = out + lax.dot_general(ah, bl, dims, preferred_element_type=F32)
    out = out + lax.dot_general(al, bh, dims, preferred_element_type=F32)
    return out


def _silu(t):
    return t * jax.nn.sigmoid(t)


def _softplus(t):
    return jnp.maximum(t, 0.0) + jnp.log1p(jnp.exp(-jnp.abs(t)))


def _log_sigmoid(t):
    return jnp.minimum(t, 0.0) - jnp.log1p(jnp.exp(-jnp.abs(t)))


def _iota2(shape, axis):
    return lax.broadcasted_iota(jnp.int32, shape, axis)


def _mm_kernel(x_ref, w_ref, o_ref, *, passes):
    if passes == 1:
        o_ref[...] = jnp.dot(x_ref[...].astype(BF16), w_ref[...].astype(BF16),
                             preferred_element_type=F32)
    else:
        o_ref[...] = _dot_hi(x_ref[...], w_ref[...])


def _matmul(x, w, *, passes=1, tm=1024):
    t, k = x.shape
    n = w.shape[1]
    tm = min(tm, t)
    tn = n if n <= 1024 else (1024 if n % 1024 == 0 else 512)
    assert t % tm == 0 and n % tn == 0
    return pl.pallas_call(
        functools.partial(_mm_kernel, passes=passes),
        grid=(t // tm, n // tn),
        in_specs=[pl.BlockSpec((tm, k), lambda i, j: (i, 0)),
                  pl.BlockSpec((k, tn), lambda i, j: (0, j))],
        out_specs=pl.BlockSpec((tm, tn), lambda i, j: (i, j)),
        out_shape=jax.ShapeDtypeStruct((t, n), F32),
        compiler_params=_cparams(("parallel", "parallel")),
        name="proj_matmul",
    )(x, w)


def _conv_kernel(prev_ref, main_ref, next_ref, w_ref, b_ref, o_ref, buf_ref, *,
                 tl, n_norm_blocks, n_scale_blocks, head_dim, scale):
    i = pl.program_id(1)
    j = pl.program_id(2)
    last = pl.num_programs(1) - 1
    keep_prev = jnp.where(i == 0, 0.0, 1.0)
    keep_next = jnp.where(i == last, 0.0, 1.0)
    buf_ref[0:8, :] = prev_ref[0, 0] * keep_prev
    buf_ref[8:8 + tl, :] = main_ref[0]
    buf_ref[8 + tl:16 + tl, :] = next_ref[0, 0] * keep_next
    w = w_ref[...]
    acc = None
    for d in range(CONV_K):
        term = buf_ref[6 + d:6 + d + tl, :] * w[d:d + 1, :]
        acc = term if acc is None else acc + term
    y = _silu(acc + b_ref[...])
    if n_norm_blocks > 0:
        do_norm = j < n_norm_blocks
        fac = jnp.where(j < n_scale_blocks, scale, 1.0)
        pieces = []
        for h in range(y.shape[1] // head_dim):
            yh = y[:, h * head_dim:(h + 1) * head_dim]
            ss = jnp.sum(yh * yh, axis=1, keepdims=True)
            r = jnp.where(do_norm, lax.rsqrt(ss + NORM_EPS) * fac, 1.0)
            pieces.append(yh * r)
        y = jnp.concatenate(pieces, axis=1)
    o_ref[0] = y


def _conv_silu(proj, lane0, width, w, bias, *, n_norm_blocks=0, n_scale_blocks=0,
               head_dim=128, scale=1.0, tc=512):
    b, l, n = proj.shape
    tl = min(512, l)
    assert l % tl == 0 and width % tc == 0 and lane0 % tc == 0 and tl % 8 == 0
    off = lane0 // tc
    halo = proj.reshape(b, l // 8, 8, n)
    nl8 = l // 8
    step8 = tl // 8
    return pl.pallas_call(
        functools.partial(_conv_kernel, tl=tl, n_norm_blocks=n_norm_blocks,
                          n_scale_blocks=n_scale_blocks, head_dim=head_dim, scale=scale),
        grid=(b, l // tl, width // tc),
        in_specs=[
            pl.BlockSpec((1, 1, 8, tc), lambda bi, i, j: (bi, jnp.maximum(i * step8 - 1, 0), 0, off + j)),
            pl.BlockSpec((1, tl, tc), lambda bi, i, j: (bi, i, off + j)),
            pl.BlockSpec((1, 1, 8, tc), lambda bi, i, j: (bi, jnp.minimum((i + 1) * step8, nl8 - 1), 0, off + j)),
            pl.BlockSpec((CONV_K, tc), lambda bi, i, j: (0, j)),
            pl.BlockSpec((1, tc), lambda bi, i, j: (0, j)),
        ],
        out_specs=pl.BlockSpec((1, tl, tc), lambda bi, i, j: (bi, i, j)),
        out_shape=jax.ShapeDtypeStruct((b, l, width), F32),
        scratch_shapes=[pltpu.VMEM((tl + 16, tc), F32)],
        compiler_params=_cparams(("parallel", "parallel", "parallel")),
        name="conv_silu",
    )(halo, proj, halo, w, bias)


GATE_ROWS = 512


def _gate_prep_kernel(raw_ref, bias_ref, alog_ref, tri_l_ref, tri_u_ref, ones_ref,
                      pf_ref, pb_ref, of_ref, ob_ref, *, aux_is_sigmoid):
    raw = raw_ref[...]
    sp = _softplus(raw + bias_ref[...])
    g = -jnp.exp(alog_ref[...]) * sp
    aux = jax.nn.sigmoid(raw) if aux_is_sigmoid else sp
    gc_l = _dot_sel_l(tri_l_ref[...], g)
    gc_u = _dot_sel_l(tri_u_ref[...], g)
    gtot = _dot_sel_l(ones_ref[...], g)
    pf = pf_ref[...]
    pb = pb_ref[...]
    of_ref[...] = (_dot_sel_r(gc_l, pf[0]) + _dot_sel_r(aux, pf[1]) + _dot_sel_r(gtot, pf[2]))
    ob_ref[...] = (_dot_sel_r(gc_u, pb[0]) + _dot_sel_r(aux, pb[1]) + _dot_sel_r(gtot, pb[2]))


def _chunk_masks(rows):
    r = np.arange(rows)
    same = (r[:, None] // CHUNK) == (r[None, :] // CHUNK)
    tri_l = same & (r[None, :] <= r[:, None])
    tri_u = same & (r[None, :] >= r[:, None])
    return (jnp.asarray(tri_l, BF16), jnp.asarray(tri_u, BF16), jnp.asarray(same, BF16))


def _placement(src_gc, src_aux, n_slab, per_slab):
    p = np.zeros((3, 64, n_slab * 128), np.float32)
    for h in range(n_slab * per_slab):
        base = (h // per_slab) * 128 + h % per_slab
        p[0, src_gc[h], base] = 1.0
        p[1, src_aux[h], base + 32] = 1.0
        p[2, src_gc[h], base + 64] = 1.0
    return jnp.asarray(p, BF16)


def _gate_prep(raw, bias64, alog64, pf, pb, *, aux_is_sigmoid):
    t = raw.shape[0]
    rows = min(GATE_ROWS, t)
    w = pf.shape[2]
    tri_l, tri_u, ones = _chunk_masks(rows)
    full = lambda shape: pl.BlockSpec(shape, lambda i: (0,) * len(shape))
    return pl.pallas_call(
        functools.partial(_gate_prep_kernel, aux_is_sigmoid=aux_is_sigmoid),
        grid=(t // rows,),
        in_specs=[pl.BlockSpec((rows, 64), lambda i: (i, 0)),
                  full((1, 64)), full((1, 64)),
                  full((rows, rows)), full((rows, rows)), full((rows, rows)),
                  full((3, 64, w)), full((3, 64, w))],
        out_specs=[pl.BlockSpec((rows, w), lambda i: (i, 0)),
                   pl.BlockSpec((rows, w), lambda i: (i, 0))],
        out_shape=[jax.ShapeDtypeStruct((t, w), F32), jax.ShapeDtypeStruct((t, w), F32)],
        compiler_params=_cparams(("parallel",)),
        name="gate_prep",
    )(raw, bias64, alog64, tri_l, tri_u, ones, pf, pb)


def _row_form(col, b, l):
    w = col.shape[1]
    return jnp.swapaxes(col.reshape(b, l // CHUNK, CHUNK, w), 2, 3)


def _tri_masks(rev):
    ri = _iota2((CHUNK, CHUNK), 0)
    ci = _iota2((CHUNK, CHUNK), 1)
    if rev:
        return ri, ci, ci >= ri, ci > ri
    return ri, ci, ci <= ri, ci < ri


GDN_SLAB = 4
GDN_SLAB_ROWS = GDN_SLAB * CHUNK


GDN_LEVELS = int(math.log2(CHUNK))
GDN_MASK_EYE, GDN_MASK_INCL, GDN_MASK_STRICT, GDN_MASK_LEVEL0 = 0, 1, 2, 3


def _gdn_masks(rev):
    n = GDN_SLAB_ROWS
    r = np.arange(n)[:, None]
    c = np.arange(n)[None, :]
    same_head = (r // CHUNK) == (c // CHUNK)
    out = [r == c, same_head & ((c >= r) if rev else (c <= r)), same_head & ((c > r) if rev else (c < r))]
    s = 1
    while s < CHUNK:
        same = (r // (2 * s)) == (c // (2 * s))
        r_hi = (r % (2 * s)) >= s
        c_hi = (c % (2 * s)) >= s
        out.append(same & ~r_hi & c_hi if rev else same & r_hi & ~c_hi)
        s *= 2
    return jnp.asarray(np.stack(out), F32)


def _gdn_scan_kernel(qf_ref, kf_ref, vf_ref, gcf_ref, grf_ref, mf_ref,
                     qb_ref, kb_ref, vb_ref, gcb_ref, grb_ref, mb_ref, of_ref, ob_ref, s_ref):
    @pl.when(pl.program_id(1) == 0)
    def _():
        s_ref[...] = jnp.zeros_like(s_ref)

    _gdn_direction(qf_ref, kf_ref, vf_ref, gcf_ref, grf_ref, mf_ref, of_ref, s_ref, 0)
    _gdn_direction(qb_ref, kb_ref, vb_ref, gcb_ref, grb_ref, mb_ref, ob_ref, s_ref, GDN_HEADS // 2)


def _gdn_direction(q_ref, k_ref, v_ref, gcol_ref, grow_ref, m_ref, o_ref, s_ref, s_base):
    c = CHUNK
    d = GDN_HEAD_DIM
    first = _iota2((c, 2 * d), 1) < d
    pair_rows = _iota2((2 * c, 2 * d), 0) < c
    pair_lanes = _iota2((2 * c, 2 * d), 1) < d
    vn_mask = pair_rows == pair_lanes
    state_mask = (_iota2((2 * d, 2 * d), 0) < d) == (_iota2((2 * d, 2 * d), 1) < d)
    slabs = range(GDN_HEADS // GDN_SLAB)
    heads = range(GDN_SLAB)

    gcols, q_sts, k_sts, ps, abs_, ts, rhss = [], [], [], [], [], [], []
    for j in slabs:
        gcol = gcol_ref[0][:, 128 * j:128 * (j + 1)]
        grow = grow_ref[0, 0][j:j + 1, :]
        qk_head = [(GDN_SLAB * j + i) // 2 for i in heads]
        q_st = jnp.concatenate([q_ref[0][:, d * h:d * (h + 1)] for h in qk_head], axis=0)
        k_st = jnp.concatenate([k_ref[0][:, d * h:d * (h + 1)] for h in qk_head], axis=0)
        v_st = jnp.concatenate([v_ref[0][:, d * (GDN_SLAB * j + i):d * (GDN_SLAB * j + i + 1)]
                                for i in heads], axis=0)
        gc_c = jnp.concatenate([gcol[:, i:i + 1] for i in heads], axis=0)
        beta_c = jnp.concatenate([gcol[:, 32 + i:33 + i] for i in heads], axis=0)
        decay = jnp.exp(jnp.minimum(gc_c - grow, 0.0)) * m_ref[GDN_MASK_INCL]
        kb = k_st.astype(BF16)
        kk = lax.dot_general(kb, kb, _NT, preferred_element_type=F32)
        ps.append(lax.dot_general(q_st.astype(BF16), kb, _NT, preferred_element_type=F32) * decay)
        a = kk * decay * m_ref[GDN_MASK_STRICT] * beta_c
        abs_.append(a.astype(BF16))
        ts.append(m_ref[GDN_MASK_EYE] - a * m_ref[GDN_MASK_LEVEL0])
        rhss.append(jnp.concatenate([v_st * beta_c, k_st * (beta_c * jnp.exp(gc_c))], axis=1))
        gcols.append(gcol)
        q_sts.append(q_st)
        k_sts.append(k_st)

    for lvl in range(1, GDN_LEVELS):
        tbs = [t.astype(BF16) for t in ts]
        xs = [jnp.dot(tb, ab, preferred_element_type=F32).astype(BF16) for tb, ab in zip(tbs, abs_)]
        ts = [t - jnp.dot(x, tb, preferred_element_type=F32) * m_ref[GDN_MASK_LEVEL0 + lvl]
              for t, x, tb in zip(ts, xs, tbs)]
    uws = [_dot(t, rhs) for t, rhs in zip(ts, rhss)]

    for j in slabs:
        gcol, q_st, k_st, p, uw = gcols[j], q_sts[j], k_sts[j], ps[j], uws[j]
        for pr in range(GDN_SLAB // 2):
            ha, hb = 2 * pr, 2 * pr + 1
            ra = slice(c * ha, c * (ha + 1))
            rb = slice(c * hb, c * (hb + 1))
            u_pair = jnp.concatenate([uw[ra, :d], uw[rb, :d]], axis=1)
            w_pair = jnp.concatenate([uw[ra, d:], uw[rb, d:]], axis=1)
            q_pair = jnp.concatenate([q_st[ra], q_st[rb]], axis=1)
            k_pair = jnp.concatenate([k_st[ra], k_st[rb]], axis=1)
            gc_pair = jnp.where(first, gcol[:, ha:ha + 1], gcol[:, hb:hb + 1])
            gt_pair = jnp.where(first, gcol[:, 64 + ha:65 + ha], gcol[:, 64 + hb:65 + hb])
            idx = (GDN_SLAB // 2) * j + pr
            s = s_ref[s_base + idx]
            wq = _dot(jnp.concatenate([w_pair, q_pair * jnp.exp(gc_pair)], axis=0), s)
            v_new = u_pair - wq[:c]
            p_pair = p[ra, 2 * c * pr:2 * c * (pr + 1)] + p[rb, 2 * c * pr:2 * c * (pr + 1)]
            vn_bd = jnp.where(vn_mask, jnp.concatenate([v_new, v_new], axis=0), 0.0)
            o_ref[0, :, 2 * d * idx:2 * d * (idx + 1)] = wq[c:] + _dot(p_pair, vn_bd)
            k_dec = k_pair * jnp.exp(gt_pair - gc_pair)
            s_ref[s_base + idx] = (s * jnp.exp(gt_pair[0:1, :])
                                   + jnp.where(state_mask, _dot(k_dec, v_new, _TN), 0.0))


def _gdn_row_form(col, b, l):
    nslab = GDN_HEADS // GDN_SLAB
    g = col.reshape(b, l // CHUNK, CHUNK, nslab, 128)[..., :GDN_SLAB]
    g = jnp.transpose(g, (0, 1, 3, 4, 2)).reshape(b, l // CHUNK, nslab, GDN_SLAB_ROWS)
    return jnp.pad(g, ((0, 0), (0, 0), (0, 8 - nslab), (0, 0)))


def _gdn_scan(qkv, col_f, row_f, col_b, row_b):
    b, l, _ = qkv.shape
    nc = l // CHUNK
    kw = GDN_HEADS // 2 * GDN_HEAD_DIM
    vw = GDN_HEADS * GDN_HEAD_DIM
    masks_f = _gdn_masks(False)
    masks_b = _gdn_masks(True)

    def specs(cidx):
        return [
            pl.BlockSpec((1, CHUNK, kw), lambda bi, n: (bi, cidx(n), 0)),
            pl.BlockSpec((1, CHUNK, kw), lambda bi, n: (bi, cidx(n), 1)),
            pl.BlockSpec((1, CHUNK, vw), lambda bi, n: (bi, cidx(n), 1)),
            pl.BlockSpec((1, CHUNK, col_f.shape[2]), lambda bi, n: (bi, cidx(n), 0)),
            pl.BlockSpec((1, 1, 8, GDN_SLAB_ROWS), lambda bi, n: (bi, cidx(n), 0, 0)),
            pl.BlockSpec(masks_f.shape, lambda bi, n: (0, 0, 0)),
        ]

    fwd = lambda n: n
    bwd = lambda n: nc - 1 - n
    out = jax.ShapeDtypeStruct((b, l, vw), F32)
    return pl.pallas_call(
        _gdn_scan_kernel,
        grid=(b, nc),
        in_specs=specs(fwd) + specs(bwd),
        out_specs=[pl.BlockSpec((1, CHUNK, vw), lambda bi, n: (bi, n, 0)),
                   pl.BlockSpec((1, CHUNK, vw), lambda bi, n: (bi, nc - 1 - n, 0))],
        out_shape=[out, out],
        scratch_shapes=[pltpu.VMEM((GDN_HEADS, 2 * GDN_HEAD_DIM, 2 * GDN_HEAD_DIM), F32)],
        compiler_params=_cparams(("parallel", "arbitrary")),
        name="gdn_scan",
    )(qkv, qkv, qkv, col_f, row_f, masks_f, qkv, qkv, qkv, col_b, row_b, masks_b)


def _ssd_scan_kernel(cf_ref, bf_ref, xf_ref, gcf_ref, grf_ref, cb_ref, bb_ref, xb_ref, gcb_ref, grb_ref,
                     e8_ref, yf_ref, yb_ref, s_ref):
    @pl.when(pl.program_id(1) == 0)
    def _():
        s_ref[...] = jnp.zeros_like(s_ref)

    _ssd_direction(cf_ref, bf_ref, xf_ref, gcf_ref, grf_ref, e8_ref, yf_ref, s_ref, 0, False)
    _ssd_direction(cb_ref, bb_ref, xb_ref, gcb_ref, grb_ref, e8_ref, yb_ref, s_ref, M2_GROUPS, True)


def _ssd_direction(c_ref, b_ref, x_ref, gcol_ref, grow_ref, e8_ref, y_ref, s_ref, s_base, rev):
    _, _, incl, _ = _tri_masks(rev)
    e8 = e8_ref[...]
    r_heads = M2_HEADS_PER_GROUP
    gw = r_heads * M2_HEAD_DIM
    lane_head = _iota2((CHUNK, gw), 1) // M2_HEAD_DIM
    groups = range(M2_GROUPS)
    ccs = [c_ref[0][:, M2_STATE * g:M2_STATE * (g + 1)] for g in groups]
    bbs = [b_ref[0][:, M2_STATE * g:M2_STATE * (g + 1)] for g in groups]
    gcols = [gcol_ref[0][:, 128 * g:128 * (g + 1)] for g in groups]
    grows = [grow_ref[0, 0][128 * g:128 * (g + 1), :] for g in groups]
    gc_xs = [_dot_sel_r(gcols[g][:, 0:r_heads], e8) for g in groups]
    dt_xs = [_dot_sel_r(gcols[g][:, 32:32 + r_heads], e8) for g in groups]
    gt_xs = [_dot_sel_r(gcols[g][:, 64:64 + r_heads], e8) for g in groups]
    us = [x_ref[0][:, gw * g:gw * (g + 1)] * dt_xs[g] for g in groups]
    scores = [_dot(ccs[g], bbs[g], _NT) for g in groups]
    states = [s_ref[s_base + g] for g in groups]
    ys = [jnp.exp(gc_xs[g]) * _dot(ccs[g], states[g]) for g in groups]
    for r in range(r_heads):
        for g in groups:
            decay = jnp.exp(jnp.where(incl, gcols[g][:, r:r + 1] - grows[g][r:r + 1, :], NEG_BIG))
            ys[g] = ys[g] + _dot(scores[g] * decay, jnp.where(lane_head == r, us[g], 0.0))
    for g in groups:
        s_ref[s_base + g] = (states[g] * jnp.exp(gt_xs[g][0:1, :])
                             + _dot(bbs[g], us[g] * jnp.exp(gt_xs[g] - gc_xs[g]), _TN))
        y_ref[0, :, gw * g:gw * (g + 1)] = ys[g]


def _ssd_scan(xbc, col_f, row_f, col_b, row_b, e8):
    b, l, _ = xbc.shape
    nc = l // CHUNK
    gw = M2_HEADS_PER_GROUP * M2_HEAD_DIM
    xw = M2_GROUPS * gw
    sw = M2_GROUPS * M2_STATE

    def specs(cidx):
        return [
            pl.BlockSpec((1, CHUNK, sw), lambda bi, n: (bi, cidx(n), xw // sw + 1)),
            pl.BlockSpec((1, CHUNK, sw), lambda bi, n: (bi, cidx(n), xw // sw)),
            pl.BlockSpec((1, CHUNK, xw), lambda bi, n: (bi, cidx(n), 0)),
            pl.BlockSpec((1, CHUNK, M2_GROUPS * 128), lambda bi, n: (bi, cidx(n), 0)),
            pl.BlockSpec((1, 1, M2_GROUPS * 128, CHUNK), lambda bi, n: (bi, cidx(n), 0, 0)),
        ]

    fwd = lambda n: n
    bwd = lambda n: nc - 1 - n
    out = jax.ShapeDtypeStruct((b, l, xw), F32)
    return pl.pallas_call(
        _ssd_scan_kernel,
        grid=(b, nc),
        in_specs=specs(fwd) + specs(bwd) + [pl.BlockSpec((M2_HEADS_PER_GROUP, gw), lambda bi, n: (0, 0))],
        out_specs=[pl.BlockSpec((1, CHUNK, xw), lambda bi, n: (bi, n, 0)),
                   pl.BlockSpec((1, CHUNK, xw), lambda bi, n: (bi, nc - 1 - n, 0))],
        out_shape=[out, out],
        scratch_shapes=[pltpu.VMEM((2 * M2_GROUPS, M2_STATE, gw), F32)],
        compiler_params=_cparams(("parallel", "arbitrary")),
        name="ssd_scan",
    )(xbc, xbc, xbc, col_f, row_f, xbc, xbc, xbc, col_b, row_b, e8)


def _gla_chunk(qs_, ks_, vs_, gs_, sts_, rev):
    c = CHUNK
    nh = len(qs_)
    hr = range(nh)
    ri, ci, incl, _ = _tri_masks(rev)
    tri = jnp.where(incl, 1.0, 0.0)
    gsums = [_dot_sel_l(tri, gs_[h]) for h in hr]
    gtots = [gs[0:1] if rev else gs[c - 1:c] for gs in gsums]
    os_ = [_dot(qs_[h] * jnp.exp(gsums[h]), sts_[h], _NT) for h in hr]
    st_new = [sts_[h] * jnp.exp(gtots[h]) + _dot(vs_[h], ks_[h] * jnp.exp(gtots[h] - gsums[h]), _TN)
              for h in hr]

    row = _iota2((c, 1), 0)
    attns = [jnp.zeros((c, c), F32) for _ in hr]
    s = c // 2
    while s >= 8:
        upper_half = (row & (2 * s - 1)) >= s
        q_rows = (~upper_half) if rev else upper_half
        shift = int(math.log2(2 * s))
        same_pair = (ri >> shift) == (ci >> shift)
        prods = []
        for h in hr:
            gsum = gsums[h]
            pieces = []
            for p in range(c // (2 * s)):
                r = p * 2 * s + (s if rev else s - 1)
                pieces.append(jnp.broadcast_to(gsum[r:r + 1], (2 * s, gsum.shape[1])))
            ref = pieces[0] if len(pieces) == 1 else jnp.concatenate(pieces, axis=0)
            e = jnp.exp(-jnp.abs(gsum - ref))
            q_s = jnp.where(q_rows, qs_[h] * e, 0.0)
            k_s = jnp.where(q_rows, 0.0, ks_[h] * e)
            prods.append(_dot(q_s, k_s, _NT))
        attns = [attns[h] + jnp.where(same_pair, prods[h], 0.0) for h in hr]
        s //= 2

    lane = _iota2((8, c), 1)
    sub = _iota2((8, 1), 0)
    blocks = [[] for _ in hr]
    for blk in range(c // 8):
        rows = slice(8 * blk, 8 * blk + 8)
        accs = [jnp.zeros((8, c), F32) for _ in hr]
        for j in range(8):
            ok = (lane == 8 * blk + j) & ((sub <= j) if rev else (sub >= j))
            for h in hr:
                gb = gsums[h][rows]
                e = jnp.exp(jnp.minimum(gb - gb[j:j + 1], 0.0))
                red = jnp.sum(qs_[h][rows] * ks_[h][rows][j:j + 1] * e, axis=1, keepdims=True)
                accs[h] = jnp.where(ok, red, accs[h])
        for h in hr:
            blocks[h].append(accs[h])
    os_ = [os_[h] + _dot(attns[h] + jnp.concatenate(blocks[h], axis=0), vs_[h]) for h in hr]
    return os_, st_new


HG_HB = 8
HG_HEADS = 8
HG_DIM = 128


def _hg_scan_kernel(q_ref, f_ref, v_ref, lb_ref, o_ref, st_ref, *, hb, rev):
    @pl.when(pl.program_id(2) == 0)
    def _():
        st_ref[...] = jnp.zeros_like(st_ref)

    lbp = lb_ref[...]
    d = HG_DIM
    qs, ks, vs, gs = [], [], [], []
    for i in range(hb):
        sl = slice(i * d, (i + 1) * d)
        f = f_ref[0][:, sl]
        a = lbp[1:2, sl]
        bterm = lbp[2:3, sl] + _log_sigmoid(f)
        qs.append(_silu(q_ref[0][:, sl]))
        ks.append(lbp[0:1, sl] * jax.nn.sigmoid(-f))
        vs.append(v_ref[0][:, sl])
        gs.append(jnp.maximum(a, bterm) + jnp.log1p(jnp.exp(-jnp.abs(a - bterm))))
    outs, st_new = _gla_chunk(qs, ks, vs, gs, [st_ref[i] for i in range(hb)], rev)
    for i in range(hb):
        st_ref[i] = st_new[i]
    o_ref[0] = jnp.concatenate(outs, axis=1)


def _hg_scan(proj, lbp, *, rev):
    b, l, _ = proj.shape
    nc = l // CHUNK
    hb = HG_HB
    nsl = HG_HEADS // hb
    f_off = (2 if rev else 1) * nsl
    cidx = (lambda n: nc - 1 - n) if rev else (lambda n: n)
    blk = (1, CHUNK, hb * HG_DIM)
    return pl.pallas_call(
        functools.partial(_hg_scan_kernel, hb=hb, rev=rev),
        grid=(b, nsl, nc),
        in_specs=[
            pl.BlockSpec(blk, lambda bi, j, n: (bi, cidx(n), j)),
            pl.BlockSpec(blk, lambda bi, j, n: (bi, cidx(n), f_off + j)),
            pl.BlockSpec(blk, lambda bi, j, n: (bi, cidx(n), 3 * nsl + j)),
            pl.BlockSpec((8, hb * HG_DIM), lambda bi, j, n: (0, j)),
        ],
        out_specs=pl.BlockSpec(blk, lambda bi, j, n: (bi, cidx(n), j)),
        out_shape=jax.ShapeDtypeStruct((b, l, HG_HEADS * HG_DIM), F32),
        scratch_shapes=[pltpu.VMEM((hb, HG_DIM, HG_DIM), F32)],
        compiler_params=_cparams(("parallel", "parallel", "arbitrary")),
        name="hgrn2_scan_bwd" if rev else "hgrn2_scan_fwd",
    )(proj, proj, proj, lbp)


GLA_HB = 4
GLA_HEADS = 4
GLA_K = 128
GLA_V = 256


def _gla_scan_kernel(q_ref, k_ref, v_ref, r_ref, wgk_ref, bgk_ref, o_ref, st_ref, *, hb, rev):
    @pl.when(pl.program_id(2) == 0)
    def _():
        st_ref[...] = jnp.zeros_like(st_ref)

    pre = _dot_hi(r_ref[0], wgk_ref[...]) + bgk_ref[...]
    gk = _log_sigmoid(pre) * (1.0 / GLA_GATE_TEMP)
    sls = [slice(i * GLA_K, (i + 1) * GLA_K) for i in range(hb)]
    outs, st_new = _gla_chunk([q_ref[0][:, sl] * (GLA_K ** -0.5) for sl in sls],
                              [k_ref[0][:, sl] for sl in sls],
                              [v_ref[0][:, i * GLA_V:(i + 1) * GLA_V] for i in range(hb)],
                              [gk[:, sl] for sl in sls],
                              [st_ref[i] for i in range(hb)], rev)
    for i in range(hb):
        st_ref[i] = st_new[i]
    o_ref[0] = jnp.concatenate(outs, axis=1)


def _gla_scan(proj, r, wgk, bgk, *, rev):
    b, l, _ = proj.shape
    nc = l // CHUNK
    hb = GLA_HB
    nsl = GLA_HEADS // hb
    cidx = (lambda n: nc - 1 - n) if rev else (lambda n: n)
    return pl.pallas_call(
        functools.partial(_gla_scan_kernel, hb=hb, rev=rev),
        grid=(b, nsl, nc),
        in_specs=[
            pl.BlockSpec((1, CHUNK, hb * GLA_K), lambda bi, j, n: (bi, cidx(n), j)),
            pl.BlockSpec((1, CHUNK, hb * GLA_K), lambda bi, j, n: (bi, cidx(n), nsl + j)),
            pl.BlockSpec((1, CHUNK, hb * GLA_V), lambda bi, j, n: (bi, cidx(n), nsl + j)),
            pl.BlockSpec((1, CHUNK, r.shape[2]), lambda bi, j, n: (bi, cidx(n), 0)),
            pl.BlockSpec((wgk.shape[0], hb * GLA_K), lambda bi, j, n: (0, j)),
            pl.BlockSpec((1, hb * GLA_K), lambda bi, j, n: (0, j)),
        ],
        out_specs=pl.BlockSpec((1, CHUNK, hb * GLA_V), lambda bi, j, n: (bi, cidx(n), j)),
        out_shape=jax.ShapeDtypeStruct((b, l, GLA_HEADS * GLA_V), F32),
        scratch_shapes=[pltpu.VMEM((hb, GLA_V, GLA_K), F32)],
        compiler_params=_cparams(("parallel", "parallel", "arbitrary")),
        name="gla_scan_bwd" if rev else "gla_scan_fwd",
    )(proj, proj, proj, r, wgk, bgk)


def _layernorm_rows(t, g, b):
    mu = jnp.mean(t, axis=1, keepdims=True)
    tc = t - mu
    var = jnp.mean(tc * tc, axis=1, keepdims=True)
    return tc * lax.rsqrt(var + LN_EPS) * g + b


def _out_ln_kernel(*refs, group, gate_first, has_skip):
    if has_skip:
        of_ref, ob_ref, z_ref, xs_ref, d_ref, nw_ref, w_ref, x_ref, g_ref, b_ref, o_ref = refs
        y = of_ref[...] + ob_ref[...] + d_ref[...] * xs_ref[...]
    else:
        of_ref, ob_ref, z_ref, nw_ref, w_ref, x_ref, g_ref, b_ref, o_ref = refs
        y = of_ref[...] + ob_ref[...]
    gate = _silu(z_ref[...])
    if gate_first:
        y = y * gate
    pieces = []
    for h in range(y.shape[1] // group):
        yh = y[:, h * group:(h + 1) * group]
        ms = jnp.mean(yh * yh, axis=1, keepdims=True)
        pieces.append(yh * lax.rsqrt(ms + NORM_EPS))
    y = jnp.concatenate(pieces, axis=1) * nw_ref[...]
    if not gate_first:
        y = y * gate
    h_out = _dot(y, w_ref[...])
    o_ref[...] = _layernorm_rows(DEEPNORM_ALPHA * x_ref[...] + h_out, g_ref[...], b_ref[...])


def _out_ln(o_f, o_b, z_arr, z_block, nw_full, w_out, x_res, ln_g, ln_b, *, group, gate_first,
            skip=None, tm=256):
    t, din = o_f.shape
    d = w_out.shape[1]
    tm = min(tm, t)
    row = lambda width, blk=0: pl.BlockSpec((tm, width), lambda i: (i, blk))
    full = lambda shape: pl.BlockSpec(shape, lambda i: (0,) * len(shape))
    ins = [o_f, o_b, z_arr]
    specs = [row(din), row(din), row(din, z_block)]
    if skip is not None:
        xs_arr, d_full = skip
        ins += [xs_arr, d_full]
        specs += [row(din), full((1, din))]
    ins += [nw_full, w_out, x_res, ln_g, ln_b]
    specs += [full((1, din)), full((din, d)), row(d), full((1, d)), full((1, d))]
    return pl.pallas_call(
        functools.partial(_out_ln_kernel, group=group, gate_first=gate_first, has_skip=skip is not None),
        grid=(t // tm,),
        in_specs=specs,
        out_specs=row(d),
        out_shape=jax.ShapeDtypeStruct((t, d), F32),
        compiler_params=_cparams(("parallel",)),
        name="gated_norm_out_ln",
    )(*ins)


def _router_kernel(x_ref, wt_ref, bias_ref, tri_ref, row_ref, col_ref, cnt_ref):
    x = x_ref[...]
    tm = x.shape[0]
    logits = _dot_hi(wt_ref[...], x, _NT) + bias_ref[...]
    sub8 = _iota2((8, tm), 0).astype(F32)
    gl = jnp.where(sub8 < MOE_GROUPS, logits[0:8], NEG_BIG)
    gmax = jnp.max(gl, axis=0, keepdims=True)
    gidx = jnp.min(jnp.where(gl == gmax, sub8, 8.0), axis=0, keepdims=True)
    gprob = 1.0 / jnp.sum(jnp.exp(gl - gmax), axis=0, keepdims=True)
    el = logits[8:16]
    for g in range(1, MOE_GROUPS):
        el = jnp.where(gidx == g, logits[8 + 8 * g:16 + 8 * g], el)
    m1 = jnp.max(el, axis=0, keepdims=True)
    i1 = jnp.min(jnp.where(el == m1, sub8, 8.0), axis=0, keepdims=True)
    el2 = jnp.where(sub8 == i1, NEG_BIG, el)
    m2 = jnp.max(el2, axis=0, keepdims=True)
    i2 = jnp.min(jnp.where(el2 == m2, sub8, 8.0), axis=0, keepdims=True)
    e2 = jnp.exp(m2 - m1)
    w1 = gprob / (1.0 + e2)
    w2 = gprob * e2 / (1.0 + e2)
    within = jnp.where(sub8 == i1, w1, 0.0) + jnp.where(sub8 == i2, w2, 0.0)
    onehot = jnp.where(sub8 == gidx, 1.0, 0.0)
    before = _dot(onehot, tri_ref[...])
    pos = jnp.sum(onehot * before, axis=0, keepdims=True)
    rec = jnp.where(sub8 == 0, gidx, jnp.where(sub8 == 1, pos, 0.0))
    rows = jnp.concatenate([within, rec], axis=0)
    row_ref[...] = rows
    padded = jnp.concatenate([rows, jnp.zeros((128 - ROUTE_ROWS, tm), F32)], axis=0)
    col_ref[...] = padded.T
    cnt_ref[0] = jnp.broadcast_to(jnp.sum(onehot, axis=1, keepdims=True), (8, 128))


def _router(x, wt, bias_col):
    t, d = x.shape
    tm = min(MOE_TILE, t)
    nt = t // tm
    r = np.arange(tm)
    tri = jnp.asarray(r[:, None] < r[None, :], BF16)
    full = lambda shape: pl.BlockSpec(shape, lambda i: (0,) * len(shape))
    return pl.pallas_call(
        _router_kernel,
        grid=(nt,),
        in_specs=[pl.BlockSpec((tm, d), lambda i: (i, 0)), full((128, d)), full((128, 1)), full((tm, tm))],
        out_specs=[pl.BlockSpec((ROUTE_ROWS, tm), lambda i: (0, i)),
                   pl.BlockSpec((tm, 128), lambda i: (i, 0)),
                   pl.BlockSpec((1, 8, 128), lambda i: (i, 0, 0))],
        out_shape=[jax.ShapeDtypeStruct((ROUTE_ROWS, t), F32),
                   jax.ShapeDtypeStruct((t, 128), F32),
                   jax.ShapeDtypeStruct((nt, 8, 128), F32)],
        compiler_params=_cparams(("parallel",)),
        name="moe_router",
    )(x, wt, bias_col, tri)


def _moe_kernel(cnt_ref, x_ref, row_ref, col_ref, wg_ref, wu_ref, wd_ref, o_ref, *, cap):
    g = pl.program_id(0)
    ti = pl.program_id(1)
    n = cnt_ref[ti * MOE_GROUPS + g]
    nb = (n + cap - 1) // cap
    tm = x_ref.shape[0]
    gf = g.astype(F32)
    rows = row_ref[...]
    cols = col_ref[...]
    in_g_row = rows[8:9, :] == gf
    pos_row = rows[9:10, :]
    in_g_col = cols[:, 8:9] == gf
    pos_col = cols[:, 9:10]
    xb = x_ref[...].astype(BF16)
    o_ref[0] = jnp.zeros((tm, o_ref.shape[2]), F32)

    def body(bi, carry):
        base = (bi * cap).astype(F32)
        slot_r = _iota2((cap, tm), 0).astype(F32) + base
        sel = jnp.where(in_g_row & (slot_r == pos_row), 1.0, 0.0).astype(BF16)
        xg = jnp.dot(sel, xb, preferred_element_type=F32).astype(BF16)
        cw = _dot_sel_l(sel, cols)
        y = jnp.zeros((cap, o_ref.shape[2]), F32)
        for e in range(MOE_EPG):
            hg = jnp.dot(xg, wg_ref[0, e], preferred_element_type=F32)
            hu = jnp.dot(xg, wu_ref[0, e], preferred_element_type=F32)
            h = _silu(hg) * hu * cw[:, e:e + 1]
            y = y + _dot(h, wd_ref[0, e])
        slot_c = _iota2((tm, cap), 1).astype(F32) + base
        sel_t = jnp.where(in_g_col & (slot_c == pos_col), 1.0, 0.0).astype(BF16)
        o_ref[0] += jnp.dot(sel_t, y.astype(BF16), preferred_element_type=F32)
        return carry

    lax.fori_loop(0, nb, body, 0)


def _moe(x, route_row, route_col, counts, wg, wu, wd):
    t, d = x.shape
    tm = min(MOE_TILE, t)
    cap = min(MOE_CAP, tm)
    nt = t // tm
    hid = wg.shape[3]
    grid_spec = pltpu.PrefetchScalarGridSpec(
        num_scalar_prefetch=1,
        grid=(MOE_GROUPS, nt),
        in_specs=[
            pl.BlockSpec((tm, d), lambda g, i, c: (i, 0)),
            pl.BlockSpec((ROUTE_ROWS, tm), lambda g, i, c: (0, i)),
            pl.BlockSpec((tm, 128), lambda g, i, c: (i, 0)),
            pl.BlockSpec((1, MOE_EPG, d, hid), lambda g, i, c: (g, 0, 0, 0)),
            pl.BlockSpec((1, MOE_EPG, d, hid), lambda g, i, c: (g, 0, 0, 0)),
            pl.BlockSpec((1, MOE_EPG, hid, d), lambda g, i, c: (g, 0, 0, 0)),
        ],
        out_specs=pl.BlockSpec((1, tm, d), lambda g, i, c: (g, i, 0)),
    )
    return pl.pallas_call(
        functools.partial(_moe_kernel, cap=cap),
        grid_spec=grid_spec,
        out_shape=jax.ShapeDtypeStruct((MOE_GROUPS, t, d), F32),
        compiler_params=_cparams(("arbitrary", "arbitrary")),
        name="moe_experts",
    )(counts, x, route_row, route_col, wg, wu, wd)


def _ln_pe_kernel(x_ref, m_ref, p_ref, g_ref, b_ref, wg_ref, wp_ref, o_ref):
    h = m_ref[0]
    for g in range(1, MOE_GROUPS):
        h = h + m_ref[g]
    y = _layernorm_rows(DEEPNORM_ALPHA * x_ref[...] + h, g_ref[...], b_ref[...])
    gate = jax.nn.sigmoid(_dot(y, wg_ref[...]))
    o_ref[...] = y + gate * _dot(p_ref[...], wp_ref[...])


def _ln_pe(x, moe_parts, p, ln_g, ln_b, w_gate, w_proj, *, tm=512):
    t, d = x.shape
    pd = p.shape[1]
    tm = min(tm, t)
    full = lambda shape: pl.BlockSpec(shape, lambda i: (0,) * len(shape))
    return pl.pallas_call(
        _ln_pe_kernel,
        grid=(t // tm,),
        in_specs=[pl.BlockSpec((tm, d), lambda i: (i, 0)),
                  pl.BlockSpec((MOE_GROUPS, tm, d), lambda i: (0, i, 0)),
                  pl.BlockSpec((tm, pd), lambda i: (i, 0)),
                  full((1, d)), full((1, d)), full((d, d)), full((pd, d))],
        out_specs=pl.BlockSpec((tm, d), lambda i: (i, 0)),
        out_shape=jax.ShapeDtypeStruct((t, d), F32),
        compiler_params=_cparams(("parallel",)),
        name="ln_input_embedding",
    )(x, moe_parts, p, ln_g, ln_b, w_gate, w_proj)


def _pad64(v):
    v = v.reshape(1, -1).astype(F32)
    return jnp.pad(v, ((0, 0), (0, 64 - v.shape[1])))


def _gdn_mixer(x2, b, l, w_in, conv_w, a_log, dt_bias, norm_w, w_out, x_res, ln_g, ln_b):
    t = b * l
    conv_ch = 4096
    main = _matmul(x2, w_in[:, :6144].astype(BF16))
    ab = _matmul(x2, w_in[:, 6144:], passes=3)
    qkv = _conv_silu(main.reshape(b, l, 6144), 0, conv_ch, conv_w.astype(F32),
                     jnp.zeros((1, conv_ch), F32), n_norm_blocks=4, n_scale_blocks=2,
                     head_dim=GDN_HEAD_DIM, scale=GDN_HEAD_DIM ** -0.5)
    per_slab = GDN_SLAB
    nslab = GDN_HEADS // per_slab
    heads = np.arange(GDN_HEADS)
    pf = _placement(heads, 32 + heads, nslab, per_slab)
    pb = _placement(16 + heads, 48 + heads, nslab, per_slab)
    col_f, col_b = _gate_prep(ab, _pad64(dt_bias), _pad64(a_log), pf, pb, aux_is_sigmoid=True)
    o_f, o_b = _gdn_scan(qkv, col_f.reshape(b, l, -1), _gdn_row_form(col_f, b, l),
                         col_b.reshape(b, l, -1), _gdn_row_form(col_b, b, l))
    nw = jnp.tile(norm_w.astype(F32), GDN_HEADS).reshape(1, -1)
    return _out_ln(o_f.reshape(t, -1), o_b.reshape(t, -1), main, 2, nw, w_out.astype(BF16),
                   x_res, ln_g, ln_b, group=GDN_HEAD_DIM, gate_first=False)


def _mamba2_mixer(x2, b, l, w_in, conv_w, conv_b, a_log, dt_bias, d_skip, norm_w, w_out,
                  x_res, ln_g, ln_b):
    t = b * l
    d_inner = 2048
    conv_ch = 3072
    main = _matmul(x2, w_in[:, :d_inner + conv_ch].astype(BF16))
    dt = _matmul(x2, w_in[:, d_inner + conv_ch:], passes=3)
    xbc = _conv_silu(main.reshape(b, l, -1), d_inner, conv_ch, conv_w.astype(F32),
                     conv_b.reshape(1, -1).astype(F32))
    heads = np.arange(M2_GROUPS * M2_HEADS_PER_GROUP)
    pf = _placement(heads, heads, M2_GROUPS, M2_HEADS_PER_GROUP)
    pb = _placement(32 + heads, 32 + heads, M2_GROUPS, M2_HEADS_PER_GROUP)
    col_f, col_b = _gate_prep(dt, _pad64(dt_bias), _pad64(a_log), pf, pb, aux_is_sigmoid=False)
    lanes = np.arange(M2_HEADS_PER_GROUP * M2_HEAD_DIM)
    e8 = jnp.asarray(np.arange(M2_HEADS_PER_GROUP)[:, None] == lanes[None, :] // M2_HEAD_DIM, BF16)
    y_f, y_b = _ssd_scan(xbc, col_f.reshape(b, l, -1), _row_form(col_f, b, l),
                         col_b.reshape(b, l, -1), _row_form(col_b, b, l), e8)
    d_full = jnp.repeat(d_skip.astype(F32), M2_HEAD_DIM).reshape(1, -1)
    return _out_ln(y_f.reshape(t, -1), y_b.reshape(t, -1), main, 0,
                   norm_w.reshape(1, -1).astype(F32), w_out.astype(BF16), x_res, ln_g, ln_b,
                   group=M2_HEADS_PER_GROUP * M2_HEAD_DIM, gate_first=True,
                   skip=(xbc.reshape(t, -1), d_full))


def _hgrn2_mixer(x2, b, l, w_in, lb, norm_w, w_out, x_res, ln_g, ln_b):
    t = b * l
    proj = _matmul(x2, w_in.astype(BF16))
    lb = lb.reshape(1, -1).astype(F32)
    lbp = jnp.concatenate([1.0 - lb, jnp.log(lb), jnp.log1p(-lb), jnp.zeros((5, lb.shape[1]), F32)], axis=0)
    p3 = proj.reshape(b, l, -1)
    o_f = _hg_scan(p3, lbp, rev=False)
    o_b = _hg_scan(p3, lbp, rev=True)
    nw = jnp.tile(norm_w.astype(F32), HG_HEADS).reshape(1, -1)
    return _out_ln(o_f.reshape(t, -1), o_b.reshape(t, -1), proj, 4, nw, w_out.astype(BF16),
                   x_res, ln_g, ln_b, group=HG_DIM, gate_first=False)


def _gla_mixer(x2, b, l, w_in, w_gk, b_gk, norm_w, w_out, x_res, ln_g, ln_b):
    t = b * l
    main = _matmul(x2, w_in[:, :3072].astype(BF16))
    r = _matmul(x2, w_in[:, 3072:], passes=3)
    p3 = main.reshape(b, l, -1)
    r3 = r.reshape(b, l, -1)
    rank = w_gk.shape[1]
    zeros = jnp.zeros_like(w_gk[0])
    w_f = jnp.concatenate([w_gk[0], zeros], axis=0).astype(F32)
    w_b = jnp.concatenate([zeros, w_gk[1]], axis=0).astype(F32)
    del rank
    o_f = _gla_scan(p3, r3, w_f, b_gk[0].reshape(1, -1).astype(F32), rev=False)
    o_b = _gla_scan(p3, r3, w_b, b_gk[1].reshape(1, -1).astype(F32), rev=True)
    nw = jnp.tile(norm_w.astype(F32), GLA_HEADS).reshape(1, -1)
    return _out_ln(o_f.reshape(t, -1), o_b.reshape(t, -1), main, 2, nw, w_out.astype(BF16),
                   x_res, ln_g, ln_b, group=GLA_V, gate_first=False)


def _hier_moe_parts(x1, w_group, b_group, w_router, b_router, w_gate, w_up, w_down):
    d = x1.shape[1]
    wt = jnp.zeros((128, d), F32)
    wt = wt.at[0:MOE_GROUPS].set(w_group.T.astype(F32))
    wt = wt.at[8:8 + MOE_GROUPS * MOE_EPG].set(w_router.T.astype(F32))
    bias = jnp.zeros((128, 1), F32)
    bias = bias.at[0:MOE_GROUPS, 0].set(b_group.astype(F32))
    bias = bias.at[8:8 + MOE_GROUPS * MOE_EPG, 0].set(b_router.astype(F32))
    route_row, route_col, cnt = _router(x1, wt, bias)
    counts = cnt[:, :MOE_GROUPS, 0].astype(jnp.int32).reshape(-1)
    return _moe(x1, route_row, route_col, counts,
                w_gate.astype(BF16), w_up.astype(BF16), w_down.astype(BF16))


def kernel(x, p, gdn_w_in, gdn_conv_w, gdn_a_log, gdn_dt_bias, gdn_norm_w, gdn_w_out, m2_w_in, m2_conv_w, m2_conv_b, m2_a_log, m2_dt_bias, m2_d, m2_norm_w, m2_w_out, hg_w_in, hg_lb_logits, hg_norm_w, hg_w_out, gla_w_in, gla_w_gk, gla_b_gk, gla_norm_w, gla_w_out, ln_g, ln_b, moe_w_group, moe_b_group, moe_w_router, moe_b_router, moe_w_gate, moe_w_up, moe_w_down, pe_w_gate, pe_w_proj):
    b, l, d = x.shape
    t = b * l
    depth = p.shape[0]
    lb_all = jnp.cumsum(jax.nn.softmax(hg_lb_logits.astype(F32), axis=0), axis=0)
    lb_all = lb_all - lb_all[0]
    xc = x.reshape(t, d).astype(F32)
    for i in range(depth):
        m, j = i % 4, i // 4
        g1 = ln_g[i, 0].reshape(1, d).astype(F32)
        b1 = ln_b[i, 0].reshape(1, d).astype(F32)
        if m == 0:
            x1 = _gdn_mixer(xc, b, l, gdn_w_in[j], gdn_conv_w[j], gdn_a_log[j], gdn_dt_bias[j],
                            gdn_norm_w[j], gdn_w_out[j], xc, g1, b1)
        elif m == 1:
            x1 = _mamba2_mixer(xc, b, l, m2_w_in[j], m2_conv_w[j], m2_conv_b[j], m2_a_log[j],
                               m2_dt_bias[j], m2_d[j], m2_norm_w[j], m2_w_out[j], xc, g1, b1)
        elif m == 2:
            x1 = _hgrn2_mixer(xc, b, l, hg_w_in[j], lb_all[i], hg_norm_w[j], hg_w_out[j], xc, g1, b1)
        else:
            x1 = _gla_mixer(xc, b, l, gla_w_in[j], gla_w_gk[j], gla_b_gk[j], gla_norm_w[j],
                            gla_w_out[j], xc, g1, b1)
        parts = _hier_moe_parts(x1, moe_w_group[i], moe_b_group[i], moe_w_router[i], moe_b_router[i],
                                moe_w_gate[i], moe_w_up[i], moe_w_down[i])
        xc = _ln_pe(x1, parts, p[i].reshape(t, -1).astype(F32),
                    ln_g[i, 1].reshape(1, d).astype(F32), ln_b[i, 1].reshape(1, d).astype(F32),
                    pe_w_gate[i].astype(BF16), pe_w_proj[i].astype(BF16))
    return xc.reshape(b, l, d).astype(x.dtype)
```
